```python
import jax, jax.numpy as jnp
from jax import lax
import numpy as np

D_MODEL = 1024
BATCH = 8
SEQ = 2048
DEPTH = 4

GRID_W = 64
CTX_LEN = 256
HEAD_DIM = 64
N_HEADS_NA = D_MODEL // (2 * HEAD_DIM)
N_HEADS_GQA = D_MODEL // (2 * HEAD_DIM)
N_KV_GQA = N_HEADS_GQA // 4
W_NA = N_HEADS_NA * HEAD_DIM
W_GQA = N_HEADS_GQA * HEAD_DIM
W_KV = N_KV_GQA * HEAD_DIM
MIX_WIDTH = W_NA + W_GQA
IN_COLS = 3 * W_NA + W_GQA + 2 * W_KV
SPLITS = (W_NA, 2 * W_NA, 3 * W_NA, 3 * W_NA + W_GQA, 3 * W_NA + W_GQA + W_KV)
MAX_WIN_H = 8
WIN_W = 16
Q_BLOCK = 128
ROPE_THETA = 10000.0
D_FF = -(-8 * D_MODEL // (3 * 256)) * 256
EPS = 1e-6

kernel_name = 'hybrid_na_gqa_dit_trunk'


def rms_norm(x, g):
    xf = x.astype(jnp.float32)
    y = xf * lax.rsqrt(jnp.mean(xf * xf, axis=-1, keepdims=True) + EPS)
    return (y * g.astype(jnp.float32)).astype(x.dtype)


def modulate(h, shift, scale):
    return h * (1 + scale) + shift


def adaln_params(cond, w_ada, b_ada):
    m = jax.nn.silu(cond) @ w_ada + b_ada
    return jnp.split(m, 6, axis=-1)


def axial_rope_tables(n_tokens):
    t = jnp.arange(n_tokens, dtype=jnp.int32)
    half = HEAD_DIM // 2
    inv_freq = ROPE_THETA ** (-jnp.arange(0, half, 2, dtype=jnp.float32) / half)
    row = (t // GRID_W).astype(jnp.float32)[:, None] * inv_freq
    col = (t % GRID_W).astype(jnp.float32)[:, None] * inv_freq
    return (jnp.cos(row), jnp.sin(row), jnp.cos(col), jnp.sin(col))


def rotate(x, cos, sin):
    x1, x2 = jnp.split(x, 2, axis=-1)
    cos = cos[:, None, :]
    sin = sin[:, None, :]
    return jnp.concatenate([x1 * cos - x2 * sin, x2 * cos + x1 * sin], axis=-1)


def apply_axial_rope(x, tabs):
    cr, sr, cc, sc = tabs
    xr, xc = jnp.split(x.astype(jnp.float32), 2, axis=-1)
    return jnp.concatenate([rotate(xr, cr, sr), rotate(xc, cc, sc)], axis=-1).astype(x.dtype)


def project_heads(h, w_in, qn_a, kn_a, qn_b, kn_b):
    B, L, _ = h.shape
    p = h @ w_in
    qa, ka, va, qb, kb, vb = jnp.split(p, SPLITS, axis=-1)
    heads = lambda t: t.reshape(B, L, -1, HEAD_DIM)
    return (rms_norm(heads(qa), qn_a), rms_norm(heads(ka), kn_a), heads(va),
            rms_norm(heads(qb), qn_b), rms_norm(heads(kb), kn_b), heads(vb))


def neighbourhood_attention(q, k, v, kc, vc, rpb):
    B, S, H, dh = q.shape
    rows = S // GRID_W
    wh = min(MAX_WIN_H, rows)
    qg = q.reshape(B, rows, GRID_W, H, dh)
    kg = k.reshape(B, rows, GRID_W, H, dh)
    vg = v.reshape(B, rows, GRID_W, H, dh)
    r = jnp.arange(rows, dtype=jnp.int32)
    rs = jnp.clip(r - wh // 2, 0, rows - wh)
    row_idx = rs[:, None] + jnp.arange(wh, dtype=jnp.int32)[None, :]
    k_rows = kg[:, row_idx]
    v_rows = vg[:, row_idx]
    cidx = jnp.arange(GRID_W, dtype=jnp.int32)
    cs = jnp.clip(cidx - WIN_W // 2, 0, GRID_W - WIN_W)
    col_ok = (cidx[None, :] >= cs[:, None]) & (cidx[None, :] < cs[:, None] + WIN_W)
    dr = row_idx - r[:, None] + (MAX_WIN_H - 1)
    dc = jnp.clip(cidx[None, :] - cidx[:, None], -(WIN_W - 1), WIN_W - 1) + (WIN_W - 1)
    bias = rpb[:, dr[:, None, :, None], dc[None, :, None, :]]
    scale = dh ** -0.5
    s_win = jnp.einsum('brqhd,brjkhd->bhrqjk', qg, k_rows,
                       preferred_element_type=jnp.float32) * scale + bias[None].astype(jnp.float32)
    s_win = jnp.where(col_ok[None, None, None, :, None, :], s_win, -jnp.inf)
    s_win = s_win.reshape(B, H, rows, GRID_W, wh * GRID_W)
    s_ctx = jnp.einsum('brqhd,bchd->bhrqc', qg, kc, preferred_element_type=jnp.float32) * scale
    p = jax.nn.softmax(jnp.concatenate([s_win, s_ctx], axis=-1), axis=-1).astype(v.dtype)
    p_win = p[..., :wh * GRID_W].reshape(B, H, rows, GRID_W, wh, GRID_W)
    p_ctx = p[..., wh * GRID_W:]
    o = (jnp.einsum('bhrqjk,brjkhd->brqhd', p_win, v_rows)
         + jnp.einsum('bhrqc,bchd->brqhd', p_ctx, vc))
    return o.reshape(B, S, H * dh)


def gqa_blocked(q, k, v, kc, vc):
    B, S, Hq, dh = q.shape
    Hkv = k.shape[2]
    g = Hq // Hkv
    k_all = jnp.concatenate([k, kc], axis=1)
    v_all = jnp.concatenate([v, vc], axis=1)
    qb = q.reshape(B, S // Q_BLOCK, Q_BLOCK, Hkv, g, dh).transpose(1, 0, 2, 3, 4, 5)
    scale = dh ** -0.5

    def block(q_blk):
        s = jnp.einsum('bqkgd,bskd->bkgqs', q_blk, k_all, preferred_element_type=jnp.float32) * scale
        p = jax.nn.softmax(s, axis=-1).astype(v_all.dtype)
        return jnp.einsum('bkgqs,bskd->bqkgd', p, v_all)

    o = lax.map(block, qb)
    return o.transpose(1, 0, 2, 3, 4, 5).reshape(B, S, Hq * dh)


def context_self_attention(q, k, v):
    B, C, Hq, dh = q.shape
    Hkv = k.shape[2]
    qg = q.reshape(B, C, Hkv, Hq // Hkv, dh)
    s = jnp.einsum('bqkgd,bskd->bkgqs', qg, k, preferred_element_type=jnp.float32) * dh ** -0.5
    p = jax.nn.softmax(s, axis=-1).astype(v.dtype)
    return jnp.einsum('bkgqs,bskd->bqkgd', p, v).reshape(B, C, Hq * dh)


def swiglu(h, w_gate, w_up, w_down):
    return (jax.nn.silu(h @ w_gate) * (h @ w_up)) @ w_down


def setup_inputs(seed: int = 0) -> dict:
    key = jax.random.key(seed)
    ks = jax.random.split(key, 18)
    nrm = lambda k, shape, s: jax.random.normal(k, shape, jnp.float32) * s
    return {
        'x': nrm(ks[0], (BATCH, SEQ, D_MODEL), 1.0),
        'c': nrm(ks[1], (BATCH, D_MODEL), 1.0),
        'ctx': nrm(ks[2], (BATCH, CTX_LEN, D_MODEL), 1.0),
        'c_ctx': nrm(ks[3], (D_MODEL,), 1.0),
        'w_ada': nrm(ks[4], (DEPTH, D_MODEL, 6 * D_MODEL), 0.5 * D_MODEL ** -0.5),
        'b_ada': nrm(ks[5], (DEPTH, 6 * D_MODEL), 0.02),
        'attn_norm': 1.0 + nrm(ks[6], (DEPTH, D_MODEL), 0.02),
        'w_in': nrm(ks[7], (DEPTH, D_MODEL, IN_COLS), D_MODEL ** -0.5),
        'q_norm_a': 1.0 + nrm(ks[8], (DEPTH, HEAD_DIM), 0.02),
        'k_norm_a': 1.0 + nrm(ks[9], (DEPTH, HEAD_DIM), 0.02),
        'q_norm_b': 1.0 + nrm(ks[10], (DEPTH, HEAD_DIM), 0.02),
        'k_norm_b': 1.0 + nrm(ks[11], (DEPTH, HEAD_DIM), 0.02),
        'rpb': nrm(ks[12], (DEPTH, N_HEADS_NA, 2 * MAX_WIN_H - 1, 2 * WIN_W - 1), 0.1),
        'w_out': nrm(ks[13], (DEPTH, MIX_WIDTH, D_MODEL), MIX_WIDTH ** -0.5),
        'ffn_norm': 1.0 + nrm(ks[14], (DEPTH, D_MODEL), 0.02),
        'w_gate': nrm(ks[15], (DEPTH, D_MODEL, D_FF), D_MODEL ** -0.5),
        'w_up': nrm(ks[16], (DEPTH, D_MODEL, D_FF), D_MODEL ** -0.5),
        'w_down': nrm(ks[17], (DEPTH, D_FF, D_MODEL), D_FF ** -0.5),
    }


def reference(x, c, ctx, c_ctx, w_ada, b_ada, attn_norm, w_in, q_norm_a, k_norm_a,
              q_norm_b, k_norm_b, rpb, w_out, ffn_norm, w_gate, w_up, w_down):
    B, S, _ = x.shape
    tabs = axial_rope_tables(S)
    for l in range(DEPTH):
        last = l == DEPTH - 1
        sh1, sc1, g1, sh2, sc2, g2 = [m[:, None, :] for m in adaln_params(c, w_ada[l], b_ada[l])]
        csh1, csc1, cg1, csh2, csc2, cg2 = adaln_params(c_ctx, w_ada[l], b_ada[l])
        h = modulate(rms_norm(x, attn_norm[l]), sh1, sc1)
        hc = modulate(rms_norm(ctx, attn_norm[l]), csh1, csc1)
        qa, ka, va, qb, kb, vb = project_heads(h, w_in[l], q_norm_a[l], k_norm_a[l], q_norm_b[l], k_norm_b[l])
        cqa, cka, cva, cqb, ckb, cvb = project_heads(hc, w_in[l], q_norm_a[l], k_norm_a[l], q_norm_b[l], k_norm_b[l])
        qb = apply_axial_rope(qb, tabs)
        kb = apply_axial_rope(kb, tabs)
        oa = neighbourhood_attention(qa, ka, va, cka, cva, rpb[l])
        ob = gqa_blocked(qb, kb, vb, ckb, cvb)
        x = x + g1 * (jnp.concatenate([oa, ob], axis=-1) @ w_out[l])
        x = x + g2 * swiglu(modulate(rms_norm(x, ffn_norm[l]), sh2, sc2), w_gate[l], w_up[l], w_down[l])
        if not last:
            oc = jnp.concatenate([context_self_attention(cqa, cka, cva),
                                  context_self_attention(cqb, ckb, cvb)], axis=-1)
            ctx = ctx + cg1 * (oc @ w_out[l])
            ctx = ctx + cg2 * swiglu(modulate(rms_norm(ctx, ffn_norm[l]), csh2, csc2),
                                     w_gate[l], w_up[l], w_down[l])
    return x
```

```python
import functools

import jax
import jax.numpy as jnp
import numpy as np
from jax import lax
from jax.experimental import pallas as pl
from jax.experimental.pallas import tpu as pltpu

D_MODEL = 1024
BATCH = 8
SEQ = 2048
DEPTH = 4
GRID_W = 64
CTX_LEN = 256
HEAD_DIM = 64
N_HEADS_NA = 8
N_HEADS_GQA = 8
N_KV_GQA = 2
GQA_GROUP = N_HEADS_GQA // N_KV_GQA
W_NA = N_HEADS_NA * HEAD_DIM
W_GQA = N_HEADS_GQA * HEAD_DIM
W_KV = N_KV_GQA * HEAD_DIM
IN_COLS = 3 * W_NA + W_GQA + 2 * W_KV
MAX_WIN_H = 8
WIN_W = 16
ROPE_THETA = 10000.0
D_FF = 2816
EPS = 1e-6

LANES = 128
T_ALL = SEQ + CTX_LEN
TILE = 256
N_LAT_TILES = SEQ // TILE
ROWS_PER_TILE = TILE // GRID_W
GRID_ROWS = SEQ // GRID_W
WIN_ROWS = 12
WIN_KEYS = WIN_ROWS * GRID_W
N_DR = 2 * MAX_WIN_H - 1
N_DC = 2 * WIN_W - 1
MASK_VALUE = -1e30
COND_ROWS = 16

QA_BLK = 0
KA_BLK = W_NA // LANES
VA_BLK = 2 * W_NA // LANES
QB_BLK = 3 * W_NA // LANES
KB_BLK = (3 * W_NA + W_GQA) // LANES
VB_BLK = KB_BLK + 1
N_BLK = IN_COLS // LANES
NORMED_BLKS = tuple(range(QA_BLK, VA_BLK)) + tuple(range(QB_BLK, VB_BLK))
ROPE_BLKS = tuple(range(QB_BLK, VB_BLK))

VMEM_LIMIT = 56 * 1024 * 1024

_BF16 = jnp.bfloat16
_F32 = jnp.float32


def _params(n_grid_dims):
    return pltpu.CompilerParams(
        dimension_semantics=("arbitrary",) * n_grid_dims,
        vmem_limit_bytes=VMEM_LIMIT)


def _split_bf16(a):
    hi = a.astype(_BF16)
    lo = (a - hi.astype(_F32)).astype(_BF16)
    return hi, lo


def _dot(a, b):
    return jnp.dot(a, b, preferred_element_type=_F32)


def _dot_nt(a, b):
    return lax.dot_general(a, b, (((1,), (1,)), ((), ())), preferred_element_type=_F32)


ADA_TN = 1024


def _adaln_kernel(cond_ref, w_ref, b_ref, o_ref):
    cond = cond_ref[...]
    a = cond / (1.0 + jnp.exp(-cond))
    a_hi, a_lo = _split_bf16(a)
    w_hi, w_lo = _split_bf16(w_ref[0])
    acc = _dot(a_hi, w_hi) + (_dot(a_lo, w_hi) + _dot(a_hi, w_lo))
    o_ref[0] = acc + b_ref[0]


def _adaln(cond, w_ada, b_ada):
    n_out = 6 * D_MODEL
    return pl.pallas_call(
        _adaln_kernel,
        grid=(DEPTH, n_out // ADA_TN),
        in_specs=[
            pl.BlockSpec((COND_ROWS, D_MODEL), lambda l, n: (0, 0)),
            pl.BlockSpec((1, D_MODEL, ADA_TN), lambda l, n: (l, 0, n)),
            pl.BlockSpec((1, 1, ADA_TN), lambda l, n: (l, 0, n)),
        ],
        out_specs=pl.BlockSpec((1, COND_ROWS, ADA_TN), lambda l, n: (l, 0, n)),
        out_shape=jax.ShapeDtypeStruct((DEPTH, COND_ROWS, n_out), _F32),
        compiler_params=_params(2),
        name="adaln",
    )(cond, w_ada, b_ada.reshape(DEPTH, 1, n_out))


def _window_dr(case, i, j):
    if case == 0:
        return j - i + (MAX_WIN_H - 1) if j < MAX_WIN_H else None
    if case == 1:
        return j - i + 3 if i <= j < i + MAX_WIN_H else None
    return j - i - 1 if j >= WIN_ROWS - MAX_WIN_H else None


def _bias_kernel(rpb_ref, o_ref):
    l = pl.program_id(0)
    h = pl.program_id(1)
    cq = lax.broadcasted_iota(jnp.int32, (GRID_W, LANES), 0)
    ck = lax.broadcasted_iota(jnp.int32, (GRID_W, LANES), 1) % GRID_W
    cs = jnp.clip(cq - WIN_W // 2, 0, GRID_W - WIN_W)
    col_ok = (ck >= cs) & (ck < cs + WIN_W)
    dc = ck - cq + (WIN_W - 1)
    base = (l * N_HEADS_NA + h) * (N_DR * N_DC)
    toeplitz = []
    for dr in range(N_DR):
        t = jnp.full((GRID_W, LANES), MASK_VALUE, _F32)
        for d in range(N_DC):
            t = jnp.where(col_ok & (dc == d), rpb_ref[base + dr * N_DC + d], t)
        toeplitz.append(t)
    masked = jnp.full((GRID_W, LANES), MASK_VALUE, _F32)
    first_half = lax.broadcasted_iota(jnp.int32, (GRID_W, LANES), 1) < GRID_W
    for case in range(3):
        for i in range(ROWS_PER_TILE):
            for jj in range(WIN_ROWS // 2):
                dr0 = _window_dr(case, i, 2 * jj)
                dr1 = _window_dr(case, i, 2 * jj + 1)
                t0 = masked if dr0 is None else toeplitz[dr0]
                t1 = masked if dr1 is None else toeplitz[dr1]
                o_ref[0, case, 0, i * GRID_W:(i + 1) * GRID_W, jj * LANES:(jj + 1) * LANES] = (
                    jnp.where(first_half, t0, t1))


def _bias_tables(rpb):
    return pl.pallas_call(
        _bias_kernel,
        grid=(DEPTH, N_HEADS_NA),
        in_specs=[pl.BlockSpec(memory_space=pltpu.SMEM)],
        out_specs=pl.BlockSpec((1, 3, 1, TILE, WIN_KEYS), lambda l, h: (l, 0, h, 0, 0)),
        out_shape=jax.ShapeDtypeStruct((DEPTH, 3, N_HEADS_NA, TILE, WIN_KEYS), _F32),
        compiler_params=_params(2),
        name="bias_tables",
    )(rpb.reshape(-1))


def _inproj_kernel(x_ref, mod_ref, nw_ref, w_ref, gain_ref, ones_ref, cos_ref, sa_ref, sb_ref,
                   o_ref):
    x = x_ref[0]
    ms = jnp.mean(x * x, axis=-1, keepdims=True)
    h = x * lax.rsqrt(ms + EPS) * nw_ref[...]
    h = h * (1.0 + mod_ref[0, 1:2, :]) + mod_ref[0, 0:1, :]
    p = _dot(h.astype(_BF16), w_ref[...])
    ones_bd = ones_ref[...]
    for blk in range(N_BLK):
        sl = slice(blk * LANES, (blk + 1) * LANES)
        y = p[:, sl]
        if blk in NORMED_BLKS:
            sq_hi, sq_lo = _split_bf16(y * y)
            ss = _dot(sq_hi, ones_bd) + _dot(sq_lo, ones_bd)
            y = y * lax.rsqrt(ss * (1.0 / HEAD_DIM) + EPS) * gain_ref[:, sl]
        if blk in ROPE_BLKS:
            y = (y * cos_ref[...]
                 + pltpu.roll(y, LANES - HEAD_DIM // 4, axis=1) * sa_ref[...]
                 + pltpu.roll(y, HEAD_DIM // 4, axis=1) * sb_ref[...])
        o_ref[0, :, sl] = y.astype(_BF16)


def _inproj(xc, mod_l, norm_w, w_in_bf, gains, ones_bd, cos_t, sa_t, sb_t):
    n_tiles = T_ALL // TILE
    mod_row = lambda b, t: (jnp.where(t >= N_LAT_TILES, BATCH, b), 0, 0)
    return pl.pallas_call(
        _inproj_kernel,
        grid=(BATCH, n_tiles),
        in_specs=[
            pl.BlockSpec((1, TILE, D_MODEL), lambda b, t: (b, t, 0)),
            pl.BlockSpec((1, 6, D_MODEL), mod_row),
            pl.BlockSpec((1, D_MODEL), lambda b, t: (0, 0)),
            pl.BlockSpec((D_MODEL, IN_COLS), lambda b, t: (0, 0)),
            pl.BlockSpec((1, IN_COLS), lambda b, t: (0, 0)),
            pl.BlockSpec((LANES, LANES), lambda b, t: (0, 0)),
            pl.BlockSpec((TILE, LANES), lambda b, t: (t, 0)),
            pl.BlockSpec((TILE, LANES), lambda b, t: (t, 0)),
            pl.BlockSpec((TILE, LANES), lambda b, t: (t, 0)),
        ],
        out_specs=pl.BlockSpec((1, TILE, IN_COLS), lambda b, t: (b, t, 0)),
        out_shape=jax.ShapeDtypeStruct((BATCH, T_ALL, IN_COLS), _BF16),
        compiler_params=_params(2),
        name="inproj",
    )(xc, mod_l, norm_w, w_in_bf, gains, ones_bd, cos_t, sa_t, sb_t)


def _half_masks():
    lane = lax.broadcasted_iota(jnp.int32, (1, LANES), 1)
    return lane < HEAD_DIM, lane >= HEAD_DIM


def _pair_attention(q, parts):
    outs = []
    for h, keep in enumerate(_half_masks()):
        qh = jnp.where(keep, q, jnp.zeros_like(q))
        scores = []
        for k, _, bias in parts:
            s = _dot_nt(qh, k)
            if bias is not None:
                s = s + bias[h]
            scores.append(s)
        m = functools.reduce(jnp.maximum, [jnp.max(s, axis=-1, keepdims=True) for s in scores])
        denom = 0.0
        acc = 0.0
        for s, (_, v, _) in zip(scores, parts):
            e = jnp.exp(s - m)
            denom = denom + jnp.sum(e, axis=-1, keepdims=True)
            acc = acc + _dot(e.astype(_BF16), v)
        outs.append(acc / denom)
    first_half, _ = _half_masks()
    return jnp.where(first_half, outs[0], outs[1])


def _na_kernel(q_ref, k_ref, v_ref, kc_ref, vc_ref, bias_ref, o_ref):
    g = pl.program_id(2)
    ctx_part = (kc_ref[0], vc_ref[0], None)

    @pl.when(g < N_LAT_TILES)
    def _():
        first_row = jnp.clip(g * ROWS_PER_TILE - MAX_WIN_H // 2, 0, GRID_ROWS - WIN_ROWS)
        start = pl.multiple_of(first_row * GRID_W, GRID_W)
        win_part = (k_ref[0, pl.ds(start, WIN_KEYS), :], v_ref[0, pl.ds(start, WIN_KEYS), :],
                    bias_ref.at[0])
        o_ref[0] = _pair_attention(q_ref[0], [win_part, ctx_part]).astype(o_ref.dtype)

    @pl.when(g >= N_LAT_TILES)
    def _():
        o_ref[0] = _pair_attention(q_ref[0], [ctx_part]).astype(o_ref.dtype)


def _na_attention(p, bias_l, n_tiles):
    bias_case = lambda g: jnp.where(g == 0, 0, jnp.where(g >= N_LAT_TILES - 1, 2, 1))
    ctx_tile = N_LAT_TILES
    return pl.pallas_call(
        _na_kernel,
        grid=(BATCH, N_HEADS_NA // 2, n_tiles),
        in_specs=[
            pl.BlockSpec((1, TILE, LANES), lambda b, hp, g: (b, g, QA_BLK + hp)),
            pl.BlockSpec((1, SEQ, LANES), lambda b, hp, g: (b, 0, KA_BLK + hp)),
            pl.BlockSpec((1, SEQ, LANES), lambda b, hp, g: (b, 0, VA_BLK + hp)),
            pl.BlockSpec((1, CTX_LEN, LANES), lambda b, hp, g: (b, ctx_tile, KA_BLK + hp)),
            pl.BlockSpec((1, CTX_LEN, LANES), lambda b, hp, g: (b, ctx_tile, VA_BLK + hp)),
            pl.BlockSpec((1, 2, TILE, WIN_KEYS), lambda b, hp, g: (bias_case(g), hp, 0, 0)),
        ],
        out_specs=pl.BlockSpec((1, TILE, LANES), lambda b, hp, g: (b, g, hp)),
        out_shape=jax.ShapeDtypeStruct((BATCH, n_tiles * TILE, W_NA), _BF16),
        compiler_params=_params(3),
        name="na_attention",
    )(p, p, p, p, p, bias_l)


def _gqa_kernel(q_ref, k_ref, v_ref, o_ref):
    t = pl.program_id(2)

    @pl.when(t < N_LAT_TILES)
    def _():
        o_ref[0] = _pair_attention(q_ref[0], [(k_ref[0], v_ref[0], None)]).astype(o_ref.dtype)

    @pl.when(t >= N_LAT_TILES)
    def _():
        part = (k_ref[0, SEQ:, :], v_ref[0, SEQ:, :], None)
        o_ref[0] = _pair_attention(q_ref[0], [part]).astype(o_ref.dtype)


def _gqa_attention(p, n_tiles):
    return pl.pallas_call(
        _gqa_kernel,
        grid=(BATCH, W_GQA // LANES, n_tiles),
        in_specs=[
            pl.BlockSpec((1, TILE, LANES), lambda b, j, t: (b, t, QB_BLK + j)),
            pl.BlockSpec((1, T_ALL, LANES), lambda b, j, t: (b, 0, KB_BLK)),
            pl.BlockSpec((1, T_ALL, LANES), lambda b, j, t: (b, 0, VB_BLK)),
        ],
        out_specs=pl.BlockSpec((1, TILE, LANES), lambda b, j, t: (b, t, j)),
        out_shape=jax.ShapeDtypeStruct((BATCH, n_tiles * TILE, W_GQA), _BF16),
        compiler_params=_params(3),
        name="gqa_attention",
    )(p, p, p)


def _post_kernel(x_ref, oa_ref, ob_ref, mod_ref, wo_ref, nw_ref, wg_ref, wu_ref, wd_ref, out_ref):
    x = x_ref[0]
    attn = _dot(oa_ref[0], wo_ref[:W_NA, :]) + _dot(ob_ref[0], wo_ref[W_NA:, :])
    x = x + mod_ref[0, 2:3, :] * attn
    ms = jnp.mean(x * x, axis=-1, keepdims=True)
    h = x * lax.rsqrt(ms + EPS) * nw_ref[...]
    h = (h * (1.0 + mod_ref[0, 4:5, :]) + mod_ref[0, 3:4, :]).astype(_BF16)
    gate = _dot(h, wg_ref[...])
    up = _dot(h, wu_ref[...])
    act = (gate / (1.0 + jnp.exp(-gate)) * up).astype(_BF16)
    out_ref[0] = x + mod_ref[0, 5:6, :] * _dot(act, wd_ref[...])


def _post(xc, oa, ob, mod_l, wo_bf, norm_w, wg_bf, wu_bf, wd_bf, n_tiles):
    mod_row = lambda b, t: (jnp.where(t >= N_LAT_TILES, BATCH, b), 0, 0)
    resident = lambda shape: pl.BlockSpec(shape, lambda b, t: (0,) * len(shape),
                                          pipeline_mode=pl.Buffered(1))
    return pl.pallas_call(
        _post_kernel,
        grid=(BATCH, n_tiles),
        in_specs=[
            pl.BlockSpec((1, TILE, D_MODEL), lambda b, t: (b, t, 0)),
            pl.BlockSpec((1, TILE, W_NA), lambda b, t: (b, t, 0)),
            pl.BlockSpec((1, TILE, W_GQA), lambda b, t: (b, t, 0)),
            pl.BlockSpec((1, 6, D_MODEL), mod_row),
            resident((W_NA + W_GQA, D_MODEL)),
            resident((1, D_MODEL)),
            resident((D_MODEL, D_FF)),
            resident((D_MODEL, D_FF)),
            resident((D_FF, D_MODEL)),
        ],
        out_specs=pl.BlockSpec((1, TILE, D_MODEL), lambda b, t: (b, t, 0)),
        out_shape=jax.ShapeDtypeStruct((BATCH, n_tiles * TILE, D_MODEL), _F32),
        compiler_params=_params(2),
        name="post",
    )(xc, oa, ob, mod_l, wo_bf, norm_w, wg_bf, wu_bf, wd_bf)


def _gqa_head_order():
    cols = []
    for j in range(W_GQA // LANES):
        for half in range(N_KV_GQA):
            head = half * GQA_GROUP + j
            cols.extend(range(head * HEAD_DIM, (head + 1) * HEAD_DIM))
    return np.asarray(cols, np.int32)


def _rope_tables():
    t = np.arange(SEQ)
    half = HEAD_DIM // 2
    inv_freq = jnp.asarray(ROPE_THETA, _F32) ** (-jnp.arange(0, half, 2, dtype=_F32) / half)
    row = jnp.asarray(t // GRID_W, _F32)[:, None] * inv_freq
    col = jnp.asarray(t % GRID_W, _F32)[:, None] * inv_freq
    quarter = HEAD_DIM // 4
    zeros = jnp.zeros((SEQ, quarter), _F32)
    cos_head = jnp.concatenate([jnp.cos(row), jnp.cos(row), jnp.cos(col), jnp.cos(col)], axis=-1)
    sa_head = jnp.concatenate([-jnp.sin(row), zeros, -jnp.sin(col), zeros], axis=-1)
    sb_head = jnp.concatenate([zeros, jnp.sin(row), zeros, jnp.sin(col)], axis=-1)
    pad = lambda a, fill: jnp.concatenate(
        [jnp.tile(a, (1, LANES // HEAD_DIM)), jnp.full((CTX_LEN, LANES), fill, _F32)], axis=0)
    return pad(cos_head, 1.0), pad(sa_head, 0.0), pad(sb_head, 0.0)


def _head_gains(qa, ka, qb, kb):
    scale = HEAD_DIM ** -0.5
    ones = jnp.ones((W_NA,), _F32)
    return jnp.concatenate([
        jnp.tile(qa * scale, N_HEADS_NA), jnp.tile(ka, N_HEADS_NA), ones,
        jnp.tile(qb * scale, N_HEADS_GQA), jnp.tile(kb, N_KV_GQA), jnp.ones((W_KV,), _F32),
    ])[None, :]


def kernel(x, c, ctx, c_ctx, w_ada, b_ada, attn_norm, w_in, q_norm_a, k_norm_a, q_norm_b,
           k_norm_b, rpb, w_out, ffn_norm, w_gate, w_up, w_down):
    assert x.shape == (BATCH, SEQ, D_MODEL) and ctx.shape == (BATCH, CTX_LEN, D_MODEL)
    cond = jnp.concatenate(
        [c, c_ctx[None, :], jnp.zeros((COND_ROWS - BATCH - 1, D_MODEL), _F32)], axis=0)
    mod = _adaln(cond, w_ada, b_ada).reshape(DEPTH, COND_ROWS, 6, D_MODEL)
    bias = _bias_tables(rpb)
    cos_t, sa_t, sb_t = _rope_tables()
    order = _gqa_head_order()
    lane = np.arange(LANES)
    ones_bd = jnp.asarray(lane[:, None] // HEAD_DIM == lane[None, :] // HEAD_DIM, _BF16)

    xc = jnp.concatenate([x, ctx], axis=1)
    for l in range(DEPTH):
        last = l == DEPTH - 1
        n_tiles = N_LAT_TILES if last else T_ALL // TILE
        w_in_l = jnp.concatenate(
            [w_in[l][:, :3 * W_NA], w_in[l][:, 3 * W_NA + order], w_in[l][:, 3 * W_NA + W_GQA:]],
            axis=1).astype(_BF16)
        w_out_l = jnp.concatenate([w_out[l][:W_NA], w_out[l][W_NA + order]], axis=0).astype(_BF16)
        gains = _head_gains(q_norm_a[l], k_norm_a[l], q_norm_b[l], k_norm_b[l])
        p = _inproj(xc, mod[l], attn_norm[l][None, :], w_in_l, gains, ones_bd, cos_t, sa_t, sb_t)
        oa = _na_attention(p, bias[l], n_tiles)
        ob = _gqa_attention(p, n_tiles)
        xc = _post(xc, oa, ob, mod[l], w_out_l, ffn_norm[l][None, :], w_gate[l].astype(_BF16),
                   w_up[l].astype(_BF16), w_down[l].astype(_BF16), n_tiles)
    return xc
```

```python
import functools

import jax
import jax.numpy as jnp
import numpy as np
from jax import lax
from jax.experimental import pallas as pl
from jax.experimental.pallas import tpu as pltpu

D_MODEL = 1024
BATCH = 8
SEQ = 2048
DEPTH = 4
GRID_W = 64
CTX_LEN = 256
HEAD_DIM = 64
N_HEADS_NA = 8
N_HEADS_GQA = 8
N_KV_GQA = 2
GQA_GROUP = N_HEADS_GQA // N_KV_GQA
W_NA = N_HEADS_NA * HEAD_DIM
W_GQA = N_HEADS_GQA * HEAD_DIM
W_KV = N_KV_GQA * HEAD_DIM
IN_COLS = 3 * W_NA + W_GQA + 2 * W_KV
MAX_WIN_H = 8
WIN_W = 16
ROPE_THETA = 10000.0
D_FF = 2816
EPS = 1e-6

LANES = 128
T_ALL = SEQ + CTX_LEN
TILE = 256
N_LAT_TILES = SEQ // TILE
ROWS_PER_TILE = TILE // GRID_W
GRID_ROWS = SEQ // GRID_W
WIN_ROWS = 12
WIN_KEYS = WIN_ROWS * GRID_W
N_DR = 2 * MAX_WIN_H - 1
N_DC = 2 * WIN_W - 1
MASK_VALUE = -1e30
LOG2_E = 1.4426950408889634
SHIFT_RANGE = 100.0
ROUNDING_SLACK = 1.01
COND_ROWS = 16

QA_BLK = 0
KA_BLK = W_NA // LANES
VA_BLK = 2 * W_NA // LANES
QB_BLK = 3 * W_NA // LANES
KB_BLK = (3 * W_NA + W_GQA) // LANES
VB_BLK = KB_BLK + 1
N_BLK = IN_COLS // LANES
NORMED_BLKS = tuple(range(QA_BLK, VA_BLK)) + tuple(range(QB_BLK, VB_BLK))
ROPE_BLKS = tuple(range(QB_BLK, VB_BLK))

VMEM_LIMIT = 56 * 1024 * 1024

_BF16 = jnp.bfloat16
_F32 = jnp.float32


def _params(n_grid_dims):
    return pltpu.CompilerParams(
        dimension_semantics=("arbitrary",) * n_grid_dims,
        vmem_limit_bytes=VMEM_LIMIT)


def _split_bf16(a):
    hi = a.astype(_BF16)
    lo = (a - hi.astype(_F32)).astype(_BF16)
    return hi, lo


def _dot(a, b):
    return jnp.dot(a, b, preferred_element_type=_F32)


def _dot_nt(a, b):
    return lax.dot_general(a, b, (((1,), (1,)), ((), ())), preferred_element_type=_F32)


ADA_TN = 1024


def _adaln_kernel(cond_ref, w_ref, b_ref, o_ref):
    cond = cond_ref[...]
    a = cond / (1.0 + jnp.exp(-cond))
    a_hi, a_lo = _split_bf16(a)
    w_hi, w_lo = _split_bf16(w_ref[0])
    acc = _dot(a_hi, w_hi) + (_dot(a_lo, w_hi) + _dot(a_hi, w_lo))
    o_ref[0] = acc + b_ref[0]


def _adaln(cond, w_ada, b_ada):
    n_out = 6 * D_MODEL
    return pl.pallas_call(
        _adaln_kernel,
        grid=(DEPTH, n_out // ADA_TN),
        in_specs=[
            pl.BlockSpec((COND_ROWS, D_MODEL), lambda l, n: (0, 0)),
            pl.BlockSpec((1, D_MODEL, ADA_TN), lambda l, n: (l, 0, n)),
            pl.BlockSpec((1, 1, ADA_TN), lambda l, n: (l, 0, n)),
        ],
        out_specs=pl.BlockSpec((1, COND_ROWS, ADA_TN), lambda l, n: (l, 0, n)),
        out_shape=jax.ShapeDtypeStruct((DEPTH, COND_ROWS, n_out), _F32),
        compiler_params=_params(2),
        name="adaln",
    )(cond, w_ada, b_ada.reshape(DEPTH, 1, n_out))


def _window_dr(case, i, j):
    if case == 0:
        return j - i + (MAX_WIN_H - 1) if j < MAX_WIN_H else None
    if case == 1:
        return j - i + 3 if i <= j < i + MAX_WIN_H else None
    return j - i - 1 if j >= WIN_ROWS - MAX_WIN_H else None


def _bias_kernel(rpb_ref, o_ref):
    l = pl.program_id(0)
    h = pl.program_id(1)
    cq = lax.broadcasted_iota(jnp.int32, (GRID_W, LANES), 0)
    ck = lax.broadcasted_iota(jnp.int32, (GRID_W, LANES), 1) % GRID_W
    cs = jnp.clip(cq - WIN_W // 2, 0, GRID_W - WIN_W)
    col_ok = (ck >= cs) & (ck < cs + WIN_W)
    dc = ck - cq + (WIN_W - 1)
    base = (l * N_HEADS_NA + h) * (N_DR * N_DC)
    toeplitz = []
    for dr in range(N_DR):
        t = jnp.full((GRID_W, LANES), MASK_VALUE, _F32)
        for d in range(N_DC):
            t = jnp.where(col_ok & (dc == d), rpb_ref[base + dr * N_DC + d] * LOG2_E, t)
        toeplitz.append(t)
    masked = jnp.full((GRID_W, LANES), MASK_VALUE, _F32)
    first_half = lax.broadcasted_iota(jnp.int32, (GRID_W, LANES), 1) < GRID_W
    for case in range(3):
        for i in range(ROWS_PER_TILE):
            for jj in range(WIN_ROWS // 2):
                dr0 = _window_dr(case, i, 2 * jj)
                dr1 = _window_dr(case, i, 2 * jj + 1)
                t0 = masked if dr0 is None else toeplitz[dr0]
                t1 = masked if dr1 is None else toeplitz[dr1]
                o_ref[0, case, 0, i * GRID_W:(i + 1) * GRID_W, jj * LANES:(jj + 1) * LANES] = (
                    jnp.where(first_half, t0, t1))


def _bias_tables(rpb):
    return pl.pallas_call(
        _bias_kernel,
        grid=(DEPTH, N_HEADS_NA),
        in_specs=[pl.BlockSpec(memory_space=pltpu.SMEM)],
        out_specs=pl.BlockSpec((1, 3, 1, TILE, WIN_KEYS), lambda l, h: (l, 0, h, 0, 0)),
        out_shape=jax.ShapeDtypeStruct((DEPTH, 3, N_HEADS_NA, TILE, WIN_KEYS), _F32),
        compiler_params=_params(2),
        name="bias_tables",
    )(rpb.reshape(-1))


def _inproj_kernel(x_ref, mod_ref, nw_ref, w_ref, gain_ref, ones_ref, cos_ref, sa_ref, sb_ref,
                   o_ref):
    x = x_ref[0]
    ms = jnp.mean(x * x, axis=-1, keepdims=True)
    h = x * lax.rsqrt(ms + EPS) * nw_ref[...]
    h = h * (1.0 + mod_ref[0, 1:2, :]) + mod_ref[0, 0:1, :]
    p = _dot(h.astype(_BF16), w_ref[...])
    ones_bd = ones_ref[...]
    for blk in range(N_BLK):
        sl = slice(blk * LANES, (blk + 1) * LANES)
        y = p[:, sl]
        if blk in NORMED_BLKS:
            sq_hi, sq_lo = _split_bf16(y * y)
            ss = _dot(sq_hi, ones_bd) + _dot(sq_lo, ones_bd)
            y = y * lax.rsqrt(ss * (1.0 / HEAD_DIM) + EPS) * gain_ref[:, sl]
        if blk in ROPE_BLKS:
            y = (y * cos_ref[...]
                 + pltpu.roll(y, LANES - HEAD_DIM // 4, axis=1) * sa_ref[...]
                 + pltpu.roll(y, HEAD_DIM // 4, axis=1) * sb_ref[...])
        o_ref[0, :, sl] = y.astype(_BF16)


def _inproj(xc, mod_l, norm_w, w_in_bf, gains, ones_bd, cos_t, sa_t, sb_t):
    n_tiles = T_ALL // TILE
    mod_row = lambda b, t: (jnp.where(t >= N_LAT_TILES, BATCH, b), 0, 0)
    return pl.pallas_call(
        _inproj_kernel,
        grid=(BATCH, n_tiles),
        in_specs=[
            pl.BlockSpec((1, TILE, D_MODEL), lambda b, t: (b, t, 0)),
            pl.BlockSpec((1, 6, D_MODEL), mod_row),
            pl.BlockSpec((1, D_MODEL), lambda b, t: (0, 0)),
            pl.BlockSpec((D_MODEL, IN_COLS), lambda b, t: (0, 0)),
            pl.BlockSpec((1, IN_COLS), lambda b, t: (0, 0)),
            pl.BlockSpec((LANES, LANES), lambda b, t: (0, 0)),
            pl.BlockSpec((TILE, LANES), lambda b, t: (t, 0)),
            pl.BlockSpec((TILE, LANES), lambda b, t: (t, 0)),
            pl.BlockSpec((TILE, LANES), lambda b, t: (t, 0)),
        ],
        out_specs=pl.BlockSpec((1, TILE, IN_COLS), lambda b, t: (b, t, 0)),
        out_shape=jax.ShapeDtypeStruct((BATCH, T_ALL, IN_COLS), _BF16),
        compiler_params=_params(2),
        name="inproj",
    )(xc, mod_l, norm_w, w_in_bf, gains, ones_bd, cos_t, sa_t, sb_t)


def _half_masks():
    lane = lax.broadcasted_iota(jnp.int32, (1, LANES), 1)
    return lane < HEAD_DIM, lane >= HEAD_DIM


def _pair_attention(q, parts, shift=None):
    n_q = q.shape[0]
    q2 = jnp.concatenate([jnp.where(keep, q, jnp.zeros_like(q)) for keep in _half_masks()], axis=0)
    scores = []
    for k, _, bias in parts:
        s = _dot_nt(q2, k)
        if bias is not None:
            s = s + bias[...].reshape(2 * n_q, -1)
        scores.append(s)
    if shift is None:
        m = functools.reduce(jnp.maximum, [jnp.max(s, axis=-1, keepdims=True) for s in scores])
    else:
        key_norm_bound, bias_max = shift
        q2f = q2.astype(_F32)
        m = jnp.sqrt(jnp.sum(q2f * q2f, axis=-1, keepdims=True)) * key_norm_bound + bias_max
    denom = 0.0
    acc = 0.0
    for s, (_, v, _) in zip(scores, parts):
        e = jnp.exp2(s - m)
        denom = denom + jnp.sum(e, axis=-1, keepdims=True)
        acc = acc + _dot(e.astype(_BF16), v)
    out = acc / denom
    first_half, _ = _half_masks()
    return jnp.where(first_half, out[:n_q], out[n_q:])


def _for_each_lane_block(width, body):
    for blk in range(width // LANES):
        body(slice(blk * LANES, (blk + 1) * LANES), blk)


def _na_kernel(shift_ref, q_ref, k_ref, v_ref, kc_ref, vc_ref, bias_ref, o_ref):
    g = pl.program_id(1)
    is_latent = g < N_LAT_TILES
    bounded = shift_ref[0] > 0.0

    def latent(shift):
        first_row = jnp.clip(g * ROWS_PER_TILE - MAX_WIN_H // 2, 0, GRID_ROWS - WIN_ROWS)
        rows = pl.ds(pl.multiple_of(first_row * GRID_W, GRID_W), WIN_KEYS)

        def body(sl, hp):
            win_part = (k_ref[0, rows, sl], v_ref[0, rows, sl], bias_ref.at[0, pl.ds(2 * hp, 2)])
            ctx_part = (kc_ref[0, :, sl], vc_ref[0, :, sl], None)
            o_ref[0, :, sl] = _pair_attention(
                q_ref[0, :, sl], [win_part, ctx_part], shift).astype(o_ref.dtype)
        _for_each_lane_block(W_NA, body)

    @pl.when(is_latent & bounded)
    def _():
        latent((shift_ref[0], shift_ref[1]))

    @pl.when(is_latent & jnp.logical_not(bounded))
    def _():
        latent(None)

    @pl.when(jnp.logical_not(is_latent))
    def _():
        def body(sl, hp):
            ctx_part = (kc_ref[0, :, sl], vc_ref[0, :, sl], None)
            o_ref[0, :, sl] = _pair_attention(q_ref[0, :, sl], [ctx_part]).astype(o_ref.dtype)
        _for_each_lane_block(W_NA, body)


def _na_attention(shift, p, bias_l, n_tiles):
    bias_case = lambda g: jnp.where(g == 0, 0, jnp.where(g >= N_LAT_TILES - 1, 2, 1))
    ctx_tile = N_LAT_TILES
    return pl.pallas_call(
        _na_kernel,
        grid=(BATCH, n_tiles),
        in_specs=[
            pl.BlockSpec(memory_space=pltpu.SMEM),
            pl.BlockSpec((1, TILE, W_NA), lambda b, g: (b, g, QA_BLK * LANES // W_NA)),
            pl.BlockSpec((1, SEQ, W_NA), lambda b, g: (b, 0, KA_BLK * LANES // W_NA)),
            pl.BlockSpec((1, SEQ, W_NA), lambda b, g: (b, 0, VA_BLK * LANES // W_NA)),
            pl.BlockSpec((1, CTX_LEN, W_NA), lambda b, g: (b, ctx_tile, KA_BLK * LANES // W_NA)),
            pl.BlockSpec((1, CTX_LEN, W_NA), lambda b, g: (b, ctx_tile, VA_BLK * LANES // W_NA)),
            pl.BlockSpec((1, N_HEADS_NA, TILE, WIN_KEYS), lambda b, g: (bias_case(g), 0, 0, 0)),
        ],
        out_specs=pl.BlockSpec((1, TILE, W_NA), lambda b, g: (b, g, 0)),
        out_shape=jax.ShapeDtypeStruct((BATCH, n_tiles * TILE, W_NA), _BF16),
        compiler_params=_params(2),
        name="na_attention",
    )(shift, p, p, p, p, p, bias_l)


def _gqa_kernel(shift_ref, q_ref, k_ref, v_ref, o_ref):
    t = pl.program_id(1)
    is_latent = t < N_LAT_TILES
    bounded = shift_ref[0] > 0.0

    def attend(part, shift):
        def body(sl, _):
            o_ref[0, :, sl] = _pair_attention(q_ref[0, :, sl], [part], shift).astype(o_ref.dtype)
        _for_each_lane_block(W_GQA, body)

    @pl.when(is_latent & bounded)
    def _():
        attend((k_ref[0], v_ref[0], None), (shift_ref[0], shift_ref[1]))

    @pl.when(is_latent & jnp.logical_not(bounded))
    def _():
        attend((k_ref[0], v_ref[0], None), None)

    @pl.when(jnp.logical_not(is_latent))
    def _():
        attend((k_ref[0, SEQ:, :], v_ref[0, SEQ:, :], None), None)


def _gqa_attention(shift, p, n_tiles):
    return pl.pallas_call(
        _gqa_kernel,
        grid=(BATCH, n_tiles),
        in_specs=[
            pl.BlockSpec(memory_space=pltpu.SMEM),
            pl.BlockSpec((1, TILE, W_GQA), lambda b, t: (b, t, QB_BLK * LANES // W_GQA)),
            pl.BlockSpec((1, T_ALL, LANES), lambda b, t: (b, 0, KB_BLK)),
            pl.BlockSpec((1, T_ALL, LANES), lambda b, t: (b, 0, VB_BLK)),
        ],
        out_specs=pl.BlockSpec((1, TILE, W_GQA), lambda b, t: (b, t, 0)),
        out_shape=jax.ShapeDtypeStruct((BATCH, n_tiles * TILE, W_GQA), _BF16),
        compiler_params=_params(2),
        name="gqa_attention",
    )(shift, p, p, p)


def _post_kernel(x_ref, oa_ref, ob_ref, mod_ref, wo_ref, nw_ref, wg_ref, wu_ref, wd_ref, out_ref):
    x = x_ref[0]
    attn = _dot(oa_ref[0], wo_ref[:W_NA, :]) + _dot(ob_ref[0], wo_ref[W_NA:, :])
    x = x + mod_ref[0, 2:3, :] * attn
    ms = jnp.mean(x * x, axis=-1, keepdims=True)
    h = x * lax.rsqrt(ms + EPS) * nw_ref[...]
    h = (h * (1.0 + mod_ref[0, 4:5, :]) + mod_ref[0, 3:4, :]).astype(_BF16)
    gate = _dot(h, wg_ref[...])
    up = _dot(h, wu_ref[...])
    act = (gate / (1.0 + jnp.exp(-gate)) * up).astype(_BF16)
    out_ref[0] = x + mod_ref[0, 5:6, :] * _dot(act, wd_ref[...])


def _post(xc, oa, ob, mod_l, wo_bf, norm_w, wg_bf, wu_bf, wd_bf, n_tiles):
    mod_row = lambda b, t: (jnp.where(t >= N_LAT_TILES, BATCH, b), 0, 0)
    resident = lambda shape: pl.BlockSpec(shape, lambda b, t: (0,) * len(shape),
                                          pipeline_mode=pl.Buffered(1))
    return pl.pallas_call(
        _post_kernel,
        grid=(BATCH, n_tiles),
        in_specs=[
            pl.BlockSpec((1, TILE, D_MODEL), lambda b, t: (b, t, 0)),
            pl.BlockSpec((1, TILE, W_NA), lambda b, t: (b, t, 0)),
            pl.BlockSpec((1, TILE, W_GQA), lambda b, t: (b, t, 0)),
            pl.BlockSpec((1, 6, D_MODEL), mod_row),
            resident((W_NA + W_GQA, D_MODEL)),
            resident((1, D_MODEL)),
            resident((D_MODEL, D_FF)),
            resident((D_MODEL, D_FF)),
            resident((D_FF, D_MODEL)),
        ],
        out_specs=pl.BlockSpec((1, TILE, D_MODEL), lambda b, t: (b, t, 0)),
        out_shape=jax.ShapeDtypeStruct((BATCH, n_tiles * TILE, D_MODEL), _F32),
        compiler_params=_params(2),
        name="post",
    )(xc, oa, ob, mod_l, wo_bf, norm_w, wg_bf, wu_bf, wd_bf)


def _gqa_head_order():
    cols = []
    for j in range(W_GQA // LANES):
        for half in range(N_KV_GQA):
            head = half * GQA_GROUP + j
            cols.extend(range(head * HEAD_DIM, (head + 1) * HEAD_DIM))
    return np.asarray(cols, np.int32)


def _rope_tables():
    t = np.arange(SEQ)
    half = HEAD_DIM // 2
    inv_freq = jnp.asarray(ROPE_THETA, _F32) ** (-jnp.arange(0, half, 2, dtype=_F32) / half)
    row = jnp.asarray(t // GRID_W, _F32)[:, None] * inv_freq
    col = jnp.asarray(t % GRID_W, _F32)[:, None] * inv_freq
    quarter = HEAD_DIM // 4
    zeros = jnp.zeros((SEQ, quarter), _F32)
    cos_head = jnp.concatenate([jnp.cos(row), jnp.cos(row), jnp.cos(col), jnp.cos(col)], axis=-1)
    sa_head = jnp.concatenate([-jnp.sin(row), zeros, -jnp.sin(col), zeros], axis=-1)
    sb_head = jnp.concatenate([zeros, jnp.sin(row), zeros, jnp.sin(col)], axis=-1)
    pad = lambda a, fill: jnp.concatenate(
        [jnp.tile(a, (1, LANES // HEAD_DIM)), jnp.full((CTX_LEN, LANES), fill, _F32)], axis=0)
    return pad(cos_head, 1.0), pad(sa_head, 0.0), pad(sb_head, 0.0)


def _head_gains(qa, ka, qb, kb):
    scale = LOG2_E * HEAD_DIM ** -0.5
    ones = jnp.ones((W_NA,), _F32)
    return jnp.concatenate([
        jnp.tile(qa * scale, N_HEADS_NA), jnp.tile(ka, N_HEADS_NA), ones,
        jnp.tile(qb * scale, N_HEADS_GQA), jnp.tile(kb, N_KV_GQA), jnp.ones((W_KV,), _F32),
    ])[None, :]


def _softmax_shift(q_gain, k_gain, bias_max):
    q_norm_bound = LOG2_E * jnp.max(jnp.abs(q_gain)) * ROUNDING_SLACK
    key_norm_bound = HEAD_DIM ** 0.5 * jnp.max(jnp.abs(k_gain)) * ROUNDING_SLACK
    worst_gap = 2.0 * q_norm_bound * key_norm_bound + bias_max
    return jnp.stack([jnp.where(worst_gap <= SHIFT_RANGE, key_norm_bound, 0.0),
                      jnp.asarray(bias_max, _F32)]).astype(_F32)


def kernel(x, c, ctx, c_ctx, w_ada, b_ada, attn_norm, w_in, q_norm_a, k_norm_a, q_norm_b,
           k_norm_b, rpb, w_out, ffn_norm, w_gate, w_up, w_down):
    assert x.shape == (BATCH, SEQ, D_MODEL) and ctx.shape == (BATCH, CTX_LEN, D_MODEL)
    cond = jnp.concatenate(
        [c, c_ctx[None, :], jnp.zeros((COND_ROWS - BATCH - 1, D_MODEL), _F32)], axis=0)
    mod = _adaln(cond, w_ada, b_ada).reshape(DEPTH, COND_ROWS, 6, D_MODEL)
    bias = _bias_tables(rpb)
    cos_t, sa_t, sb_t = _rope_tables()
    order = _gqa_head_order()
    lane = np.arange(LANES)
    ones_bd = jnp.asarray(lane[:, None] // HEAD_DIM == lane[None, :] // HEAD_DIM, _BF16)

    xc = jnp.concatenate([x, ctx], axis=1)
    for l in range(DEPTH):
        last = l == DEPTH - 1
        n_tiles = N_LAT_TILES if last else T_ALL // TILE
        w_in_l = jnp.concatenate(
            [w_in[l][:, :3 * W_NA], w_in[l][:, 3 * W_NA + order], w_in[l][:, 3 * W_NA + W_GQA:]],
            axis=1).astype(_BF16)
        w_out_l = jnp.concatenate([w_out[l][:W_NA], w_out[l][W_NA + order]], axis=0).astype(_BF16)
        gains = _head_gains(q_norm_a[l], k_norm_a[l], q_norm_b[l], k_norm_b[l])
        p = _inproj(xc, mod[l], attn_norm[l][None, :], w_in_l, gains, ones_bd, cos_t, sa_t, sb_t)
        bias_max = LOG2_E * jnp.maximum(jnp.max(rpb[l]), 0.0)
        oa = _na_attention(_softmax_shift(q_norm_a[l], k_norm_a[l], bias_max), p, bias[l], n_tiles)
        ob = _gqa_attention(_softmax_shift(q_norm_b[l], k_norm_b[l], 0.0), p, n_tiles)
        xc = _post(xc, oa, ob, mod[l], w_out_l, ffn_norm[l][None, :], w_gate[l].astype(_BF16),
                   w_up[l].astype(_BF16), w_down[l].astype(_BF16), n_tiles)
    return xc
```

```python
import functools

import jax
import jax.numpy as jnp
import numpy as np
from jax import lax
from jax.experimental import pallas as pl
from jax.experimental.pallas import tpu as pltpu

D_MODEL = 1024
BATCH = 8
SEQ = 2048
DEPTH = 4
GRID_W = 64
CTX_LEN = 256
HEAD_DIM = 64
N_HEADS_NA = 8
N_HEADS_GQA = 8
N_KV_GQA = 2
GQA_GROUP = N_HEADS_GQA // N_KV_GQA
W_NA = N_HEADS_NA * HEAD_DIM
W_GQA = N_HEADS_GQA * HEAD_DIM
W_KV = N_KV_GQA * HEAD_DIM
IN_COLS = 3 * W_NA + W_GQA + 2 * W_KV
MAX_WIN_H = 8
WIN_W = 16
ROPE_THETA = 10000.0
D_FF = 2816
EPS = 1e-6

LANES = 128
T_ALL = SEQ + CTX_LEN
TILE = 256
N_LAT_TILES = SEQ // TILE
ROWS_PER_TILE = TILE // GRID_W
GRID_ROWS = SEQ // GRID_W
WIN_ROWS = 12
WIN_KEYS = WIN_ROWS * GRID_W
N_DR = 2 * MAX_WIN_H - 1
N_DC = 2 * WIN_W - 1
MASK_VALUE = -1e30
LOG2_E = 1.4426950408889634
SHIFT_RANGE = 100.0
ROUNDING_SLACK = 1.01
COND_ROWS = 16

QA_BLK = 0
KA_BLK = W_NA // LANES
VA_BLK = 2 * W_NA // LANES
QB_BLK = 3 * W_NA // LANES
KB_BLK = (3 * W_NA + W_GQA) // LANES
VB_BLK = KB_BLK + 1
N_BLK = IN_COLS // LANES
PROJ_COLS = IN_COLS + 2 * W_KV
NORMED_BLKS = tuple(range(QA_BLK, VA_BLK)) + tuple(range(QB_BLK, VB_BLK))
ROPE_BLKS = tuple(range(QB_BLK, VB_BLK))

VMEM_LIMIT = 56 * 1024 * 1024

_BF16 = jnp.bfloat16
_F32 = jnp.float32


def _params(n_grid_dims):
    return pltpu.CompilerParams(
        dimension_semantics=("arbitrary",) * n_grid_dims,
        vmem_limit_bytes=VMEM_LIMIT)


def _split_bf16(a):
    hi = a.astype(_BF16)
    lo = (a - hi.astype(_F32)).astype(_BF16)
    return hi, lo


def _dot(a, b):
    return jnp.dot(a, b, preferred_element_type=_F32)


def _dot_nt(a, b):
    return lax.dot_general(a, b, (((1,), (1,)), ((), ())), preferred_element_type=_F32)


ADA_TN = 1024


def _adaln_kernel(cond_ref, w_ref, b_ref, o_ref):
    cond = cond_ref[...]
    a = cond / (1.0 + jnp.exp(-cond))
    a_hi, a_lo = _split_bf16(a)
    w_hi, w_lo = _split_bf16(w_ref[0])
    acc = _dot(a_hi, w_hi) + (_dot(a_lo, w_hi) + _dot(a_hi, w_lo))
    o_ref[0] = acc + b_ref[0]


def _adaln(cond, w_ada, b_ada):
    n_out = 6 * D_MODEL
    return pl.pallas_call(
        _adaln_kernel,
        grid=(DEPTH, n_out // ADA_TN),
        in_specs=[
            pl.BlockSpec((COND_ROWS, D_MODEL), lambda l, n: (0, 0)),
            pl.BlockSpec((1, D_MODEL, ADA_TN), lambda l, n: (l, 0, n)),
            pl.BlockSpec((1, 1, ADA_TN), lambda l, n: (l, 0, n)),
        ],
        out_specs=pl.BlockSpec((1, COND_ROWS, ADA_TN), lambda l, n: (l, 0, n)),
        out_shape=jax.ShapeDtypeStruct((DEPTH, COND_ROWS, n_out), _F32),
        compiler_params=_params(2),
        name="adaln",
    )(cond, w_ada, b_ada.reshape(DEPTH, 1, n_out))


def _window_dr(case, i, j):
    if case == 0:
        return j - i + (MAX_WIN_H - 1) if j < MAX_WIN_H else None
    if case == 1:
        return j - i + 3 if i <= j < i + MAX_WIN_H else None
    return j - i - 1 if j >= WIN_ROWS - MAX_WIN_H else None


def _bias_kernel(rpb_ref, o_ref):
    l = pl.program_id(0)
    h = pl.program_id(1)
    cq = lax.broadcasted_iota(jnp.int32, (GRID_W, LANES), 0)
    ck = lax.broadcasted_iota(jnp.int32, (GRID_W, LANES), 1) % GRID_W
    cs = jnp.clip(cq - WIN_W // 2, 0, GRID_W - WIN_W)
    col_ok = (ck >= cs) & (ck < cs + WIN_W)
    dc = ck - cq + (WIN_W - 1)
    base = (l * N_HEADS_NA + h) * (N_DR * N_DC)
    toeplitz = []
    for dr in range(N_DR):
        t = jnp.full((GRID_W, LANES), MASK_VALUE, _F32)
        for d in range(N_DC):
            t = jnp.where(col_ok & (dc == d), rpb_ref[base + dr * N_DC + d] * LOG2_E, t)
        toeplitz.append(t)
    masked = jnp.full((GRID_W, LANES), MASK_VALUE, _F32)
    first_half = lax.broadcasted_iota(jnp.int32, (GRID_W, LANES), 1) < GRID_W
    for case in range(3):
        for i in range(ROWS_PER_TILE):
            for jj in range(WIN_ROWS // 2):
                dr0 = _window_dr(case, i, 2 * jj)
                dr1 = _window_dr(case, i, 2 * jj + 1)
                t0 = masked if dr0 is None else toeplitz[dr0]
                t1 = masked if dr1 is None else toeplitz[dr1]
                o_ref[0, case, 0, i * GRID_W:(i + 1) * GRID_W, jj * LANES:(jj + 1) * LANES] = (
                    jnp.where(first_half, t0, t1))


def _bias_tables(rpb):
    return pl.pallas_call(
        _bias_kernel,
        grid=(DEPTH, N_HEADS_NA),
        in_specs=[pl.BlockSpec(memory_space=pltpu.SMEM)],
        out_specs=pl.BlockSpec((1, 3, 1, TILE, WIN_KEYS), lambda l, h: (l, 0, h, 0, 0)),
        out_shape=jax.ShapeDtypeStruct((DEPTH, 3, N_HEADS_NA, TILE, WIN_KEYS), _F32),
        compiler_params=_params(2),
        name="bias_tables",
    )(rpb.reshape(-1))


def _stream_tile(lat_ref, ctx_ref):
    return jnp.where(pl.program_id(1) >= N_LAT_TILES, ctx_ref[0], lat_ref[0])


def _stream_specs(ctx_tile):
    return [pl.BlockSpec((1, TILE, D_MODEL), lambda b, t: (b, jnp.minimum(t, N_LAT_TILES - 1), 0)),
            pl.BlockSpec((1, TILE, D_MODEL), lambda b, t: (b, ctx_tile, 0))]


def _inproj_kernel(x_ref, xctx_ref, mod_ref, nw_ref, w_ref, gain_ref, ones_ref, cos_ref, sa_ref,
                   sb_ref, o_ref):
    x = _stream_tile(x_ref, xctx_ref)
    ms = jnp.mean(x * x, axis=-1, keepdims=True)
    h = x * lax.rsqrt(ms + EPS) * nw_ref[...]
    h = h * (1.0 + mod_ref[0, 1:2, :]) + mod_ref[0, 0:1, :]
    p = _dot(h.astype(_BF16), w_ref[...])
    ones_bd = ones_ref[...]
    for blk in range(N_BLK):
        sl = slice(blk * LANES, (blk + 1) * LANES)
        y = p[:, sl]
        if blk in NORMED_BLKS:
            sq_hi, sq_lo = _split_bf16(y * y)
            ss = _dot(sq_hi, ones_bd) + _dot(sq_lo, ones_bd)
            y = y * lax.rsqrt(ss * (1.0 / HEAD_DIM) + EPS) * gain_ref[:, sl]
        if blk in ROPE_BLKS:
            y = (y * cos_ref[...]
                 + pltpu.roll(y, LANES - HEAD_DIM // 4, axis=1) * sa_ref[...]
                 + pltpu.roll(y, HEAD_DIM // 4, axis=1) * sb_ref[...])
        if blk < KB_BLK:
            o_ref[0, :, sl] = y.astype(_BF16)
        else:
            swapped = pltpu.roll(y, HEAD_DIM, axis=1)
            first_half, _ = _half_masks()
            out_blk = KB_BLK + N_KV_GQA * (blk - KB_BLK)
            for head, dup in enumerate((jnp.where(first_half, y, swapped),
                                        jnp.where(first_half, swapped, y))):
                o_ref[0, :, (out_blk + head) * LANES:(out_blk + head + 1) * LANES] = dup.astype(_BF16)


def _inproj(x_lat, x_ctx, ctx_tile, mod_l, norm_w, w_in_bf, layer, gains, ones_bd, cos_t, sa_t, sb_t):
    n_tiles = T_ALL // TILE
    mod_row = lambda b, t: (jnp.where(t >= N_LAT_TILES, BATCH, b), 0, 0)
    return pl.pallas_call(
        _inproj_kernel,
        grid=(BATCH, n_tiles),
        in_specs=[
            *_stream_specs(ctx_tile),
            pl.BlockSpec((1, 6, D_MODEL), mod_row),
            pl.BlockSpec((1, D_MODEL), lambda b, t: (0, 0)),
            pl.BlockSpec((None, D_MODEL, IN_COLS), lambda b, t: (layer, 0, 0)),
            pl.BlockSpec((1, IN_COLS), lambda b, t: (0, 0)),
            pl.BlockSpec((LANES, LANES), lambda b, t: (0, 0)),
            pl.BlockSpec((TILE, LANES), lambda b, t: (t, 0)),
            pl.BlockSpec((TILE, LANES), lambda b, t: (t, 0)),
            pl.BlockSpec((TILE, LANES), lambda b, t: (t, 0)),
        ],
        out_specs=pl.BlockSpec((1, TILE, PROJ_COLS), lambda b, t: (b, t, 0)),
        out_shape=jax.ShapeDtypeStruct((BATCH, T_ALL, PROJ_COLS), _BF16),
        compiler_params=_params(2),
        name="inproj",
    )(x_lat, x_ctx, mod_l, norm_w, w_in_bf, gains, ones_bd, cos_t, sa_t, sb_t)


def _half_masks():
    lane = lax.broadcasted_iota(jnp.int32, (1, LANES), 1)
    return lane < HEAD_DIM, lane >= HEAD_DIM


def _pair_attention(q, parts, shift=None):
    n_q = q.shape[0]
    q2 = jnp.concatenate([jnp.where(keep, q, jnp.zeros_like(q)) for keep in _half_masks()], axis=0)
    scores = []
    for k, _, bias in parts:
        s = _dot_nt(q2, k)
        if bias is not None:
            s = s + bias[...].reshape(2 * n_q, -1)
        scores.append(s)
    if shift is None:
        m = functools.reduce(jnp.maximum, [jnp.max(s, axis=-1, keepdims=True) for s in scores])
    else:
        key_norm_bound, bias_max = shift
        q2f = q2.astype(_F32)
        m = jnp.sqrt(jnp.sum(q2f * q2f, axis=-1, keepdims=True)) * key_norm_bound + bias_max
    denom = 0.0
    acc = 0.0
    for s, (_, v, _) in zip(scores, parts):
        e = jnp.exp2(s - m)
        denom = denom + jnp.sum(e, axis=-1, keepdims=True)
        acc = acc + _dot(e.astype(_BF16), v)
    out = acc / denom
    first_half, _ = _half_masks()
    return jnp.where(first_half, out[:n_q], out[n_q:])


def _for_each_lane_block(width, body):
    for blk in range(width // LANES):
        body(slice(blk * LANES, (blk + 1) * LANES), blk)


def _na_kernel(shift_ref, q_ref, k_ref, v_ref, kc_ref, vc_ref, bias_ref, o_ref):
    g = pl.program_id(1)
    is_latent = g < N_LAT_TILES
    bounded = shift_ref[0] > 0.0

    def latent(shift):
        first_row = jnp.clip(g * ROWS_PER_TILE - MAX_WIN_H // 2, 0, GRID_ROWS - WIN_ROWS)
        rows = pl.ds(pl.multiple_of(first_row * GRID_W, GRID_W), WIN_KEYS)
        case = jnp.where(g == 0, 0, jnp.where(g == N_LAT_TILES - 1, 2, 1))

        def body(sl, hp):
            win_part = (k_ref[0, rows, sl], v_ref[0, rows, sl], bias_ref.at[case, pl.ds(2 * hp, 2)])
            ctx_part = (kc_ref[0, :, sl], vc_ref[0, :, sl], None)
            o_ref[0, :, sl] = _pair_attention(
                q_ref[0, :, sl], [win_part, ctx_part], shift).astype(o_ref.dtype)
        _for_each_lane_block(W_NA, body)

    @pl.when(is_latent & bounded)
    def _():
        latent((shift_ref[0], shift_ref[1]))

    @pl.when(is_latent & jnp.logical_not(bounded))
    def _():
        latent(None)

    @pl.when(jnp.logical_not(is_latent))
    def _():
        def body(sl, hp):
            ctx_part = (kc_ref[0, :, sl], vc_ref[0, :, sl], None)
            o_ref[0, :, sl] = _pair_attention(q_ref[0, :, sl], [ctx_part]).astype(o_ref.dtype)
        _for_each_lane_block(W_NA, body)


def _na_attention(shift, p, bias, layer, n_tiles):
    ctx_tile = N_LAT_TILES
    return pl.pallas_call(
        _na_kernel,
        grid=(BATCH, n_tiles),
        in_specs=[
            pl.BlockSpec(memory_space=pltpu.SMEM),
            pl.BlockSpec((1, TILE, W_NA), lambda b, g: (b, g, QA_BLK * LANES // W_NA)),
            pl.BlockSpec((1, SEQ, W_NA), lambda b, g: (b, 0, KA_BLK * LANES // W_NA)),
            pl.BlockSpec((1, SEQ, W_NA), lambda b, g: (b, 0, VA_BLK * LANES // W_NA)),
            pl.BlockSpec((1, CTX_LEN, W_NA), lambda b, g: (b, ctx_tile, KA_BLK * LANES // W_NA)),
            pl.BlockSpec((1, CTX_LEN, W_NA), lambda b, g: (b, ctx_tile, VA_BLK * LANES // W_NA)),
            pl.BlockSpec((None, 3, N_HEADS_NA, TILE, WIN_KEYS), lambda b, g: (layer, 0, 0, 0, 0),
                         pipeline_mode=pl.Buffered(1)),
        ],
        out_specs=pl.BlockSpec((1, TILE, W_NA), lambda b, g: (b, g, 0)),
        out_shape=jax.ShapeDtypeStruct((BATCH, n_tiles * TILE, W_NA), _BF16),
        compiler_params=_params(2),
        name="na_attention",
    )(shift, p, p, p, p, p, bias)


def _gqa_kernel(shift_ref, q_ref, *refs):
    k_refs, v_refs, o_ref = refs[:N_KV_GQA], refs[N_KV_GQA:2 * N_KV_GQA], refs[-1]
    t = pl.program_id(1)
    is_latent = t < N_LAT_TILES
    bounded = shift_ref[0] > 0.0
    blocks_per_kv = W_GQA // LANES // N_KV_GQA

    def attend(rows, shift):
        def body(sl, j):
            kv = j // blocks_per_kv
            part = (k_refs[kv][0, rows, :], v_refs[kv][0, rows, :], None)
            o_ref[0, :, sl] = _pair_attention(q_ref[0, :, sl], [part], shift).astype(o_ref.dtype)
        _for_each_lane_block(W_GQA, body)

    @pl.when(is_latent & bounded)
    def _():
        attend(slice(None), (shift_ref[0], shift_ref[1]))

    @pl.when(is_latent & jnp.logical_not(bounded))
    def _():
        attend(slice(None), None)

    @pl.when(jnp.logical_not(is_latent))
    def _():
        attend(slice(SEQ, T_ALL), None)


def _gqa_attention(shift, p, n_tiles):
    kv_spec = lambda blk: pl.BlockSpec((1, T_ALL, LANES), lambda b, t: (b, 0, blk))
    return pl.pallas_call(
        _gqa_kernel,
        grid=(BATCH, n_tiles),
        in_specs=[
            pl.BlockSpec(memory_space=pltpu.SMEM),
            pl.BlockSpec((1, TILE, W_GQA), lambda b, t: (b, t, QB_BLK * LANES // W_GQA)),
            *[kv_spec(KB_BLK + head) for head in range(N_KV_GQA)],
            *[kv_spec(KB_BLK + N_KV_GQA + head) for head in range(N_KV_GQA)],
        ],
        out_specs=pl.BlockSpec((1, TILE, W_GQA), lambda b, t: (b, t, 0)),
        out_shape=jax.ShapeDtypeStruct((BATCH, n_tiles * TILE, W_GQA), _BF16),
        compiler_params=_params(2),
        name="gqa_attention",
    )(shift, p, *([p] * (2 * N_KV_GQA)))


def _post_kernel(x_ref, xctx_ref, oa_ref, ob_ref, mod_ref, wo_ref, nw_ref, wg_ref, wu_ref, wd_ref,
                 out_ref):
    x = _stream_tile(x_ref, xctx_ref)
    attn = _dot(oa_ref[0], wo_ref[:W_NA, :]) + _dot(ob_ref[0], wo_ref[W_NA:, :])
    x = x + mod_ref[0, 2:3, :] * attn
    ms = jnp.mean(x * x, axis=-1, keepdims=True)
    h = x * lax.rsqrt(ms + EPS) * nw_ref[...]
    h = (h * (1.0 + mod_ref[0, 4:5, :]) + mod_ref[0, 3:4, :]).astype(_BF16)
    gate = _dot(h, wg_ref[...])
    up = _dot(h, wu_ref[...])
    act = (gate / (1.0 + jnp.exp(-gate)) * up).astype(_BF16)
    out_ref[0] = x + mod_ref[0, 5:6, :] * _dot(act, wd_ref[...])


def _post(x_lat, x_ctx, ctx_tile, oa, ob, mod_l, wo_bf, norm_w, wg_bf, wu_bf, wd_bf, layer, n_tiles):
    mod_row = lambda b, t: (jnp.where(t >= N_LAT_TILES, BATCH, b), 0, 0)
    resident = lambda shape: pl.BlockSpec((None,) + shape, lambda b, t: (layer,) + (0,) * len(shape),
                                          pipeline_mode=pl.Buffered(1))
    return pl.pallas_call(
        _post_kernel,
        grid=(BATCH, n_tiles),
        in_specs=[
            *_stream_specs(ctx_tile),
            pl.BlockSpec((1, TILE, W_NA), lambda b, t: (b, t, 0)),
            pl.BlockSpec((1, TILE, W_GQA), lambda b, t: (b, t, 0)),
            pl.BlockSpec((1, 6, D_MODEL), mod_row),
            resident((W_NA + W_GQA, D_MODEL)),
            resident((1, D_MODEL)),
            resident((D_MODEL, D_FF)),
            resident((D_MODEL, D_FF)),
            resident((D_FF, D_MODEL)),
        ],
        out_specs=pl.BlockSpec((1, TILE, D_MODEL), lambda b, t: (b, t, 0)),
        out_shape=jax.ShapeDtypeStruct((BATCH, n_tiles * TILE, D_MODEL), _F32),
        compiler_params=_params(2),
        name="post",
    )(x_lat, x_ctx, oa, ob, mod_l, wo_bf, norm_w, wg_bf, wu_bf, wd_bf)


def _rope_tables():
    t = np.arange(SEQ)
    half = HEAD_DIM // 2
    inv_freq = np.float32(ROPE_THETA) ** (-np.arange(0, half, 2, dtype=np.float32) / np.float32(half))
    row = (t // GRID_W).astype(np.float32)[:, None] * inv_freq
    col = (t % GRID_W).astype(np.float32)[:, None] * inv_freq
    zeros = np.zeros((SEQ, HEAD_DIM // 4), np.float32)
    cos_head = np.concatenate([np.cos(row), np.cos(row), np.cos(col), np.cos(col)], axis=-1)
    sa_head = np.concatenate([-np.sin(row), zeros, -np.sin(col), zeros], axis=-1)
    sb_head = np.concatenate([zeros, np.sin(row), zeros, np.sin(col)], axis=-1)
    pad = lambda a, fill: jnp.asarray(np.concatenate(
        [np.tile(a, (1, LANES // HEAD_DIM)), np.full((CTX_LEN, LANES), fill, np.float32)],
        axis=0).astype(np.float32))
    return pad(cos_head, 1.0), pad(sa_head, 0.0), pad(sb_head, 0.0)


def _head_gains(qa, ka, qb, kb):
    scale = LOG2_E * HEAD_DIM ** -0.5
    ones = jnp.ones((W_NA,), _F32)
    return jnp.concatenate([
        jnp.tile(qa * scale, N_HEADS_NA), jnp.tile(ka, N_HEADS_NA), ones,
        jnp.tile(qb * scale, N_HEADS_GQA), jnp.tile(kb, N_KV_GQA), jnp.ones((W_KV,), _F32),
    ])[None, :]


def _softmax_shift(q_gain, k_gain, bias_max):
    q_norm_bound = LOG2_E * jnp.max(jnp.abs(q_gain)) * ROUNDING_SLACK
    key_norm_bound = HEAD_DIM ** 0.5 * jnp.max(jnp.abs(k_gain)) * ROUNDING_SLACK
    worst_gap = 2.0 * q_norm_bound * key_norm_bound + bias_max
    return jnp.stack([jnp.where(worst_gap <= SHIFT_RANGE, key_norm_bound, 0.0),
                      jnp.asarray(bias_max, _F32)]).astype(_F32)


def kernel(x, c, ctx, c_ctx, w_ada, b_ada, attn_norm, w_in, q_norm_a, k_norm_a, q_norm_b,
           k_norm_b, rpb, w_out, ffn_norm, w_gate, w_up, w_down):
    assert x.shape == (BATCH, SEQ, D_MODEL) and ctx.shape == (BATCH, CTX_LEN, D_MODEL)
    cond = jnp.concatenate(
        [c, c_ctx[None, :], jnp.zeros((COND_ROWS - BATCH - 1, D_MODEL), _F32)], axis=0)
    mod = _adaln(cond, w_ada, b_ada).reshape(DEPTH, COND_ROWS, 6, D_MODEL)
    bias = _bias_tables(rpb)
    cos_t, sa_t, sb_t = _rope_tables()
    lane = np.arange(LANES)
    ones_bd = jnp.asarray(lane[:, None] // HEAD_DIM == lane[None, :] // HEAD_DIM, _BF16)
    w_in_bf, w_out_bf = w_in.astype(_BF16), w_out.astype(_BF16)
    w_gate_bf, w_up_bf, w_down_bf = w_gate.astype(_BF16), w_up.astype(_BF16), w_down.astype(_BF16)

    stream = (x, ctx, 0)
    for l in range(DEPTH):
        last = l == DEPTH - 1
        n_tiles = N_LAT_TILES if last else T_ALL // TILE
        gains = _head_gains(q_norm_a[l], k_norm_a[l], q_norm_b[l], k_norm_b[l])
        p = _inproj(*stream, mod[l], attn_norm[l][None, :], w_in_bf, l, gains, ones_bd, cos_t, sa_t, sb_t)
        bias_max = LOG2_E * jnp.maximum(jnp.max(rpb[l]), 0.0)
        oa = _na_attention(_softmax_shift(q_norm_a[l], k_norm_a[l], bias_max), p, bias, l, n_tiles)
        ob = _gqa_attention(_softmax_shift(q_norm_b[l], k_norm_b[l], 0.0), p, n_tiles)
        xc = _post(*stream, oa, ob, mod[l], w_out_bf, ffn_norm.reshape(DEPTH, 1, D_MODEL), w_gate_bf,
                   w_up_bf, w_down_bf, l, n_tiles)
        stream = (xc, xc, N_LAT_TILES)
    return xc
```

```python
import functools

import jax
import jax.numpy as jnp
import numpy as np
from jax import lax
from jax.experimental import pallas as pl
from jax.experimental.pallas import tpu as pltpu

D_MODEL = 1024
BATCH = 8
SEQ = 2048
DEPTH = 4
GRID_W = 64
CTX_LEN = 256
HEAD_DIM = 64
N_HEADS_NA = 8
N_HEADS_GQA = 8
N_KV_GQA = 2
GQA_GROUP = N_HEADS_GQA // N_KV_GQA
W_NA = N_HEADS_NA * HEAD_DIM
W_GQA = N_HEADS_GQA * HEAD_DIM
W_KV = N_KV_GQA * HEAD_DIM
IN_COLS = 3 * W_NA + W_GQA + 2 * W_KV
MAX_WIN_H = 8
WIN_W = 16
ROPE_THETA = 10000.0
D_FF = 2816
EPS = 1e-6

LANES = 128
T_ALL = SEQ + CTX_LEN
TILE = 256
N_LAT_TILES = SEQ // TILE
ROWS_PER_TILE = TILE // GRID_W
GRID_ROWS = SEQ // GRID_W
WIN_ROWS = 12
WIN_KEYS = WIN_ROWS * GRID_W
N_DR = 2 * MAX_WIN_H - 1
N_DC = 2 * WIN_W - 1
MASK_VALUE = -1e30
LOG2_E = 1.4426950408889634
SHIFT_RANGE = 100.0
ROUNDING_SLACK = 1.01
COND_ROWS = 16

QA_BLK = 0
KA_BLK = W_NA // LANES
VA_BLK = 2 * W_NA // LANES
QB_BLK = 3 * W_NA // LANES
KB_BLK = (3 * W_NA + W_GQA) // LANES
VB_BLK = KB_BLK + 1
N_BLK = IN_COLS // LANES
PROJ_COLS = IN_COLS + 2 * W_KV
NORMED_BLKS = tuple(range(QA_BLK, VA_BLK)) + tuple(range(QB_BLK, VB_BLK))
ROPE_BLKS = tuple(range(QB_BLK, VB_BLK))

VMEM_LIMIT = 56 * 1024 * 1024

_BF16 = jnp.bfloat16
_F32 = jnp.float32


def _params(n_grid_dims):
    return pltpu.CompilerParams(
        dimension_semantics=("arbitrary",) * n_grid_dims,
        vmem_limit_bytes=VMEM_LIMIT)


def _split_bf16(a):
    hi = a.astype(_BF16)
    lo = (a - hi.astype(_F32)).astype(_BF16)
    return hi, lo


def _dot(a, b):
    return jnp.dot(a, b, preferred_element_type=_F32)


def _dot_nt(a, b):
    return lax.dot_general(a, b, (((1,), (1,)), ((), ())), preferred_element_type=_F32)


ADA_TN = 1024


def _adaln_kernel(cond_ref, w_ref, b_ref, o_ref):
    cond = cond_ref[...]
    a = cond / (1.0 + jnp.exp(-cond))
    a_hi, a_lo = _split_bf16(a)
    w_hi, w_lo = _split_bf16(w_ref[0])
    acc = _dot(a_hi, w_hi) + (_dot(a_lo, w_hi) + _dot(a_hi, w_lo))
    o_ref[0] = acc + b_ref[0]


def _adaln(cond, w_ada, b_ada):
    n_out = 6 * D_MODEL
    return pl.pallas_call(
        _adaln_kernel,
        grid=(DEPTH, n_out // ADA_TN),
        in_specs=[
            pl.BlockSpec((COND_ROWS, D_MODEL), lambda l, n: (0, 0)),
            pl.BlockSpec((1, D_MODEL, ADA_TN), lambda l, n: (l, 0, n)),
            pl.BlockSpec((1, 1, ADA_TN), lambda l, n: (l, 0, n)),
        ],
        out_specs=pl.BlockSpec((1, COND_ROWS, ADA_TN), lambda l, n: (l, 0, n)),
        out_shape=jax.ShapeDtypeStruct((DEPTH, COND_ROWS, n_out), _F32),
        compiler_params=_params(2),
        name="adaln",
    )(cond, w_ada, b_ada.reshape(DEPTH, 1, n_out))


def _window_dr(case, i, j):
    if case == 0:
        return j - i + (MAX_WIN_H - 1) if j < MAX_WIN_H else None
    if case == 1:
        return j - i + 3 if i <= j < i + MAX_WIN_H else None
    return j - i - 1 if j >= WIN_ROWS - MAX_WIN_H else None


def _bias_kernel(rpb_ref, o_ref):
    l = pl.program_id(0)
    h = pl.program_id(1)
    cq = lax.broadcasted_iota(jnp.int32, (GRID_W, LANES), 0)
    ck = lax.broadcasted_iota(jnp.int32, (GRID_W, LANES), 1) % GRID_W
    cs = jnp.clip(cq - WIN_W // 2, 0, GRID_W - WIN_W)
    col_ok = (ck >= cs) & (ck < cs + WIN_W)
    dc = ck - cq + (WIN_W - 1)
    base = (l * N_HEADS_NA + h) * (N_DR * N_DC)
    toeplitz = []
    for dr in range(N_DR):
        t = jnp.full((GRID_W, LANES), MASK_VALUE, _F32)
        for d in range(N_DC):
            t = jnp.where(col_ok & (dc == d), rpb_ref[base + dr * N_DC + d] * LOG2_E, t)
        toeplitz.append(t)
    masked = jnp.full((GRID_W, LANES), MASK_VALUE, _F32)
    first_half = lax.broadcasted_iota(jnp.int32, (GRID_W, LANES), 1) < GRID_W
    for case in range(3):
        for i in range(ROWS_PER_TILE):
            for jj in range(WIN_ROWS // 2):
                dr0 = _window_dr(case, i, 2 * jj)
                dr1 = _window_dr(case, i, 2 * jj + 1)
                t0 = masked if dr0 is None else toeplitz[dr0]
                t1 = masked if dr1 is None else toeplitz[dr1]
                o_ref[0, case, 0, i * GRID_W:(i + 1) * GRID_W, jj * LANES:(jj + 1) * LANES] = (
                    jnp.where(first_half, t0, t1))


def _bias_tables(rpb):
    return pl.pallas_call(
        _bias_kernel,
        grid=(DEPTH, N_HEADS_NA),
        in_specs=[pl.BlockSpec(memory_space=pltpu.SMEM)],
        out_specs=pl.BlockSpec((1, 3, 1, TILE, WIN_KEYS), lambda l, h: (l, 0, h, 0, 0)),
        out_shape=jax.ShapeDtypeStruct((DEPTH, 3, N_HEADS_NA, TILE, WIN_KEYS), _F32),
        compiler_params=_params(2),
        name="bias_tables",
    )(rpb.reshape(-1))


def _stream_tile(stream_refs):
    if len(stream_refs) == 1:
        return stream_refs[0][0]
    lat_ref, ctx_ref = stream_refs
    return jnp.where(pl.program_id(1) >= N_LAT_TILES, ctx_ref[0], lat_ref[0])


def _stream_specs(n_streams):
    if n_streams == 1:
        return [pl.BlockSpec((1, TILE, D_MODEL), lambda b, t: (b, t, 0))]
    return [pl.BlockSpec((1, TILE, D_MODEL), lambda b, t: (b, jnp.minimum(t, N_LAT_TILES - 1), 0)),
            pl.BlockSpec((1, TILE, D_MODEL), lambda b, t: (b, 0, 0))]


def _inproj_kernel(n_streams, *refs):
    mod_ref, nw_ref, w_ref, gain_ref, ones_ref, cos_ref, sa_ref, sb_ref, o_ref = refs[n_streams:]
    x = _stream_tile(refs[:n_streams])
    ms = jnp.mean(x * x, axis=-1, keepdims=True)
    h = x * lax.rsqrt(ms + EPS) * nw_ref[...]
    h = h * (1.0 + mod_ref[0, 1:2, :]) + mod_ref[0, 0:1, :]
    p = _dot(h.astype(_BF16), w_ref[...])
    ones_bd = ones_ref[...]
    for blk in range(N_BLK):
        sl = slice(blk * LANES, (blk + 1) * LANES)
        y = p[:, sl]
        if blk in NORMED_BLKS:
            sq_hi, sq_lo = _split_bf16(y * y)
            ss = _dot(sq_hi, ones_bd) + _dot(sq_lo, ones_bd)
            y = y * lax.rsqrt(ss * (1.0 / HEAD_DIM) + EPS) * gain_ref[:, sl]
        if blk in ROPE_BLKS:
            y = (y * cos_ref[...]
                 + pltpu.roll(y, LANES - HEAD_DIM // 4, axis=1) * sa_ref[...]
                 + pltpu.roll(y, HEAD_DIM // 4, axis=1) * sb_ref[...])
        if blk < KB_BLK:
            o_ref[0, :, sl] = y.astype(_BF16)
        else:
            swapped = pltpu.roll(y, HEAD_DIM, axis=1)
            first_half, _ = _half_masks()
            out_blk = KB_BLK + N_KV_GQA * (blk - KB_BLK)
            for head, dup in enumerate((jnp.where(first_half, y, swapped),
                                        jnp.where(first_half, swapped, y))):
                o_ref[0, :, (out_blk + head) * LANES:(out_blk + head + 1) * LANES] = dup.astype(_BF16)


def _inproj(streams, mod_l, norm_w, w_in_bf, layer, gains, ones_bd, cos_t, sa_t, sb_t):
    n_tiles = T_ALL // TILE
    mod_row = lambda b, t: (jnp.where(t >= N_LAT_TILES, BATCH, b), 0, 0)
    return pl.pallas_call(
        functools.partial(_inproj_kernel, len(streams)),
        grid=(BATCH, n_tiles),
        in_specs=[
            *_stream_specs(len(streams)),
            pl.BlockSpec((1, 6, D_MODEL), mod_row),
            pl.BlockSpec((1, D_MODEL), lambda b, t: (0, 0)),
            pl.BlockSpec((None, D_MODEL, IN_COLS), lambda b, t: (layer, 0, 0)),
            pl.BlockSpec((1, IN_COLS), lambda b, t: (0, 0)),
            pl.BlockSpec((LANES, LANES), lambda b, t: (0, 0)),
            pl.BlockSpec((TILE, LANES), lambda b, t: (t, 0)),
            pl.BlockSpec((TILE, LANES), lambda b, t: (t, 0)),
            pl.BlockSpec((TILE, LANES), lambda b, t: (t, 0)),
        ],
        out_specs=pl.BlockSpec((1, TILE, PROJ_COLS), lambda b, t: (b, t, 0)),
        out_shape=jax.ShapeDtypeStruct((BATCH, T_ALL, PROJ_COLS), _BF16),
        compiler_params=_params(2),
        name="inproj",
    )(*streams, mod_l, norm_w, w_in_bf, gains, ones_bd, cos_t, sa_t, sb_t)


def _half_masks():
    lane = lax.broadcasted_iota(jnp.int32, (1, LANES), 1)
    return lane < HEAD_DIM, lane >= HEAD_DIM


def _pair_attention(q, parts, shift=None):
    n_q = q.shape[0]
    q2 = jnp.concatenate([jnp.where(keep, q, jnp.zeros_like(q)) for keep in _half_masks()], axis=0)
    scores = []
    for k, _, bias in parts:
        s = _dot_nt(q2, k)
        if bias is not None:
            s = s + bias[...].reshape(2 * n_q, -1)
        scores.append(s)
    if shift is None:
        m = functools.reduce(jnp.maximum, [jnp.max(s, axis=-1, keepdims=True) for s in scores])
    else:
        key_norm_bound, bias_max = shift
        q2f = q2.astype(_F32)
        m = jnp.sqrt(jnp.sum(q2f * q2f, axis=-1, keepdims=True)) * key_norm_bound + bias_max
    denom = 0.0
    acc = 0.0
    for s, (_, v, _) in zip(scores, parts):
        e = jnp.exp2(s - m)
        denom = denom + jnp.sum(e, axis=-1, keepdims=True)
        acc = acc + _dot(e.astype(_BF16), v)
    out = acc / denom
    first_half, _ = _half_masks()
    return jnp.where(first_half, out[:n_q], out[n_q:])


def _for_each_lane_block(width, body):
    for blk in range(width // LANES):
        body(slice(blk * LANES, (blk + 1) * LANES), blk)


def _na_kernel(shift_ref, q_ref, k_ref, v_ref, kc_ref, vc_ref, bias_ref, o_ref):
    g = pl.program_id(1)
    is_latent = g < N_LAT_TILES
    bounded = shift_ref[0] > 0.0

    def latent(shift):
        first_row = jnp.clip(g * ROWS_PER_TILE - MAX_WIN_H // 2, 0, GRID_ROWS - WIN_ROWS)
        rows = pl.ds(pl.multiple_of(first_row * GRID_W, GRID_W), WIN_KEYS)
        case = jnp.where(g == 0, 0, jnp.where(g == N_LAT_TILES - 1, 2, 1))

        def body(sl, hp):
            win_part = (k_ref[0, rows, sl], v_ref[0, rows, sl], bias_ref.at[case, pl.ds(2 * hp, 2)])
            ctx_part = (kc_ref[0, :, sl], vc_ref[0, :, sl], None)
            o_ref[0, :, sl] = _pair_attention(
                q_ref[0, :, sl], [win_part, ctx_part], shift).astype(o_ref.dtype)
        _for_each_lane_block(W_NA, body)

    @pl.when(is_latent & bounded)
    def _():
        latent((shift_ref[0], shift_ref[1]))

    @pl.when(is_latent & jnp.logical_not(bounded))
    def _():
        latent(None)

    @pl.when(jnp.logical_not(is_latent))
    def _():
        def body(sl, hp):
            ctx_part = (kc_ref[0, :, sl], vc_ref[0, :, sl], None)
            o_ref[0, :, sl] = _pair_attention(q_ref[0, :, sl], [ctx_part]).astype(o_ref.dtype)
        _for_each_lane_block(W_NA, body)


def _na_attention(shift, p, bias, layer, n_tiles):
    ctx_tile = N_LAT_TILES
    return pl.pallas_call(
        _na_kernel,
        grid=(BATCH, n_tiles),
        in_specs=[
            pl.BlockSpec(memory_space=pltpu.SMEM),
            pl.BlockSpec((1, TILE, W_NA), lambda b, g: (b, g, QA_BLK * LANES // W_NA)),
            pl.BlockSpec((1, SEQ, W_NA), lambda b, g: (b, 0, KA_BLK * LANES // W_NA)),
            pl.BlockSpec((1, SEQ, W_NA), lambda b, g: (b, 0, VA_BLK * LANES // W_NA)),
            pl.BlockSpec((1, CTX_LEN, W_NA), lambda b, g: (b, ctx_tile, KA_BLK * LANES // W_NA)),
            pl.BlockSpec((1, CTX_LEN, W_NA), lambda b, g: (b, ctx_tile, VA_BLK * LANES // W_NA)),
            pl.BlockSpec((None, 3, N_HEADS_NA, TILE, WIN_KEYS), lambda b, g: (layer, 0, 0, 0, 0),
                         pipeline_mode=pl.Buffered(1)),
        ],
        out_specs=pl.BlockSpec((1, TILE, W_NA), lambda b, g: (b, g, 0)),
        out_shape=jax.ShapeDtypeStruct((BATCH, n_tiles * TILE, W_NA), _BF16),
        compiler_params=_params(2),
        name="na_attention",
    )(shift, p, p, p, p, p, bias)


def _gqa_kernel(shift_ref, q_ref, *refs):
    k_refs, v_refs, o_ref = refs[:N_KV_GQA], refs[N_KV_GQA:2 * N_KV_GQA], refs[-1]
    t = pl.program_id(1)
    is_latent = t < N_LAT_TILES
    bounded = shift_ref[0] > 0.0
    blocks_per_kv = W_GQA // LANES // N_KV_GQA

    def attend(rows, shift):
        def body(sl, j):
            kv = j // blocks_per_kv
            part = (k_refs[kv][0, rows, :], v_refs[kv][0, rows, :], None)
            o_ref[0, :, sl] = _pair_attention(q_ref[0, :, sl], [part], shift).astype(o_ref.dtype)
        _for_each_lane_block(W_GQA, body)

    @pl.when(is_latent & bounded)
    def _():
        attend(slice(None), (shift_ref[0], shift_ref[1]))

    @pl.when(is_latent & jnp.logical_not(bounded))
    def _():
        attend(slice(None), None)

    @pl.when(jnp.logical_not(is_latent))
    def _():
        attend(slice(SEQ, T_ALL), None)


def _gqa_attention(shift, p, n_tiles):
    kv_spec = lambda blk: pl.BlockSpec((1, T_ALL, LANES), lambda b, t: (b, 0, blk))
    return pl.pallas_call(
        _gqa_kernel,
        grid=(BATCH, n_tiles),
        in_specs=[
            pl.BlockSpec(memory_space=pltpu.SMEM),
            pl.BlockSpec((1, TILE, W_GQA), lambda b, t: (b, t, QB_BLK * LANES // W_GQA)),
            *[kv_spec(KB_BLK + head) for head in range(N_KV_GQA)],
            *[kv_spec(KB_BLK + N_KV_GQA + head) for head in range(N_KV_GQA)],
        ],
        out_specs=pl.BlockSpec((1, TILE, W_GQA), lambda b, t: (b, t, 0)),
        out_shape=jax.ShapeDtypeStruct((BATCH, n_tiles * TILE, W_GQA), _BF16),
        compiler_params=_params(2),
        name="gqa_attention",
    )(shift, p, *([p] * (2 * N_KV_GQA)))


def _post_kernel(n_streams, *refs):
    oa_ref, ob_ref, mod_ref, wo_ref, nw_ref, wg_ref, wu_ref, wd_ref, out_ref = refs[n_streams:]
    x = _stream_tile(refs[:n_streams])
    attn = _dot(oa_ref[0], wo_ref[:W_NA, :]) + _dot(ob_ref[0], wo_ref[W_NA:, :])
    x = x + mod_ref[0, 2:3, :] * attn
    ms = jnp.mean(x * x, axis=-1, keepdims=True)
    h = x * lax.rsqrt(ms + EPS) * nw_ref[...]
    h = (h * (1.0 + mod_ref[0, 4:5, :]) + mod_ref[0, 3:4, :]).astype(_BF16)
    gate = _dot(h, wg_ref[...])
    up = _dot(h, wu_ref[...])
    act = (gate / (1.0 + jnp.exp(-gate)) * up).astype(_BF16)
    out_ref[0] = x + mod_ref[0, 5:6, :] * _dot(act, wd_ref[...])


def _post(streams, oa, ob, mod_l, wo_bf, norm_w, wg_bf, wu_bf, wd_bf, layer, n_tiles):
    mod_row = lambda b, t: (jnp.where(t >= N_LAT_TILES, BATCH, b), 0, 0)
    resident = lambda shape: pl.BlockSpec((None,) + shape, lambda b, t: (layer,) + (0,) * len(shape),
                                          pipeline_mode=pl.Buffered(1))
    return pl.pallas_call(
        functools.partial(_post_kernel, len(streams)),
        grid=(BATCH, n_tiles),
        in_specs=[
            *_stream_specs(len(streams)),
            pl.BlockSpec((1, TILE, W_NA), lambda b, t: (b, t, 0)),
            pl.BlockSpec((1, TILE, W_GQA), lambda b, t: (b, t, 0)),
            pl.BlockSpec((1, 6, D_MODEL), mod_row),
            resident((W_NA + W_GQA, D_MODEL)),
            resident((1, D_MODEL)),
            resident((D_MODEL, D_FF)),
            resident((D_MODEL, D_FF)),
            resident((D_FF, D_MODEL)),
        ],
        out_specs=pl.BlockSpec((1, TILE, D_MODEL), lambda b, t: (b, t, 0)),
        out_shape=jax.ShapeDtypeStruct((BATCH, n_tiles * TILE, D_MODEL), _F32),
        compiler_params=_params(2),
        name="post",
    )(*streams, oa, ob, mod_l, wo_bf, norm_w, wg_bf, wu_bf, wd_bf)


def _rope_tables():
    t = np.arange(SEQ)
    half = HEAD_DIM // 2
    inv_freq = np.float32(ROPE_THETA) ** (-np.arange(0, half, 2, dtype=np.float32) / np.float32(half))
    row = (t // GRID_W).astype(np.float32)[:, None] * inv_freq
    col = (t % GRID_W).astype(np.float32)[:, None] * inv_freq
    zeros = np.zeros((SEQ, HEAD_DIM // 4), np.float32)
    cos_head = np.concatenate([np.cos(row), np.cos(row), np.cos(col), np.cos(col)], axis=-1)
    sa_head = np.concatenate([-np.sin(row), zeros, -np.sin(col), zeros], axis=-1)
    sb_head = np.concatenate([zeros, np.sin(row), zeros, np.sin(col)], axis=-1)
    pad = lambda a, fill: jnp.asarray(np.concatenate(
        [np.tile(a, (1, LANES // HEAD_DIM)), np.full((CTX_LEN, LANES), fill, np.float32)],
        axis=0).astype(np.float32))
    return pad(cos_head, 1.0), pad(sa_head, 0.0), pad(sb_head, 0.0)


def _head_gains(qa, ka, qb, kb):
    scale = LOG2_E * HEAD_DIM ** -0.5
    ones = jnp.ones((W_NA,), _F32)
    return jnp.concatenate([
        jnp.tile(qa * scale, N_HEADS_NA), jnp.tile(ka, N_HEADS_NA), ones,
        jnp.tile(qb * scale, N_HEADS_GQA), jnp.tile(kb, N_KV_GQA), jnp.ones((W_KV,), _F32),
    ])[None, :]


def _softmax_shift(q_gain, k_gain, bias_max):
    q_norm_bound = LOG2_E * jnp.max(jnp.abs(q_gain)) * ROUNDING_SLACK
    key_norm_bound = HEAD_DIM ** 0.5 * jnp.max(jnp.abs(k_gain)) * ROUNDING_SLACK
    worst_gap = 2.0 * q_norm_bound * key_norm_bound + bias_max
    return jnp.stack([jnp.where(worst_gap <= SHIFT_RANGE, key_norm_bound, 0.0),
                      jnp.asarray(bias_max, _F32)]).astype(_F32)


def kernel(x, c, ctx, c_ctx, w_ada, b_ada, attn_norm, w_in, q_norm_a, k_norm_a, q_norm_b,
           k_norm_b, rpb, w_out, ffn_norm, w_gate, w_up, w_down):
    assert x.shape == (BATCH, SEQ, D_MODEL) and ctx.shape == (BATCH, CTX_LEN, D_MODEL)
    cond = jnp.concatenate(
        [c, c_ctx[None, :], jnp.zeros((COND_ROWS - BATCH - 1, D_MODEL), _F32)], axis=0)
    mod = _adaln(cond, w_ada, b_ada).reshape(DEPTH, COND_ROWS, 6, D_MODEL)
    bias = _bias_tables(rpb)
    cos_t, sa_t, sb_t = _rope_tables()
    lane = np.arange(LANES)
    ones_bd = jnp.asarray(lane[:, None] // HEAD_DIM == lane[None, :] // HEAD_DIM, _BF16)
    w_in_bf, w_out_bf = w_in.astype(_BF16), w_out.astype(_BF16)
    w_gate_bf, w_up_bf, w_down_bf = w_gate.astype(_BF16), w_up.astype(_BF16), w_down.astype(_BF16)

    streams = (x, ctx)
    for l in range(DEPTH):
        last = l == DEPTH - 1
        n_tiles = N_LAT_TILES if last else T_ALL // TILE
        gains = _head_gains(q_norm_a[l], k_norm_a[l], q_norm_b[l], k_norm_b[l])
        p = _inproj(streams, mod[l], attn_norm[l][None, :], w_in_bf, l, gains, ones_bd, cos_t, sa_t, sb_t)
        bias_max = LOG2_E * jnp.maximum(jnp.max(rpb[l]), 0.0)
        oa = _na_attention(_softmax_shift(q_norm_a[l], k_norm_a[l], bias_max), p, bias, l, n_tiles)
        ob = _gqa_attention(_softmax_shift(q_norm_b[l], k_norm_b[l], 0.0), p, n_tiles)
        xc = _post(streams, oa, ob, mod[l], w_out_bf, ffn_norm.reshape(DEPTH, 1, D_MODEL), w_gate_bf,
                   w_up_bf, w_down_bf, l, n_tiles)
        streams = (xc,)
    return xc
```

```python
import functools

import jax
import jax.numpy as jnp
import numpy as np
from jax import lax
from jax.experimental import pallas as pl
from jax.experimental.pallas import tpu as pltpu

D_MODEL = 1024
BATCH = 8
SEQ = 2048
DEPTH = 4
GRID_W = 64
CTX_LEN = 256
HEAD_DIM = 64
N_HEADS_NA = 8
N_HEADS_GQA = 8
N_KV_GQA = 2
GQA_GROUP = N_HEADS_GQA // N_KV_GQA
W_NA = N_HEADS_NA * HEAD_DIM
W_GQA = N_HEADS_GQA * HEAD_DIM
W_KV = N_KV_GQA * HEAD_DIM
IN_COLS = 3 * W_NA + W_GQA + 2 * W_KV
MAX_WIN_H = 8
WIN_W = 16
ROPE_THETA = 10000.0
D_FF = 2816
EPS = 1e-6

LANES = 128
T_ALL = SEQ + CTX_LEN
TILE = 256
N_LAT_TILES = SEQ // TILE
ROWS_PER_TILE = TILE // GRID_W
GRID_ROWS = SEQ // GRID_W
WIN_ROWS = 12
WIN_KEYS = WIN_ROWS * GRID_W
N_DR = 2 * MAX_WIN_H - 1
N_DC = 2 * WIN_W - 1
MASK_VALUE = -1e30
LOG2_E = 1.4426950408889634
SHIFT_RANGE = 100.0
ROUNDING_SLACK = 1.01
COND_ROWS = 16

QA_BLK = 0
KA_BLK = W_NA // LANES
VA_BLK = 2 * W_NA // LANES
QB_BLK = 3 * W_NA // LANES
KB_BLK = (3 * W_NA + W_GQA) // LANES
VB_BLK = KB_BLK + 1
N_BLK = IN_COLS // LANES
PROJ_COLS = IN_COLS + 2 * W_KV
NORMED_BLKS = tuple(range(QA_BLK, VA_BLK)) + tuple(range(QB_BLK, VB_BLK))
ROPE_BLKS = tuple(range(QB_BLK, VB_BLK))

VMEM_LIMIT = 56 * 1024 * 1024

_BF16 = jnp.bfloat16
_F32 = jnp.float32


def _params(n_grid_dims):
    return pltpu.CompilerParams(
        dimension_semantics=("arbitrary",) * n_grid_dims,
        vmem_limit_bytes=VMEM_LIMIT)


def _split_bf16(a):
    hi = a.astype(_BF16)
    lo = (a - hi.astype(_F32)).astype(_BF16)
    return hi, lo


def _dot(a, b):
    return jnp.dot(a, b, preferred_element_type=_F32)


def _dot_nt(a, b):
    return lax.dot_general(a, b, (((1,), (1,)), ((), ())), preferred_element_type=_F32)


ADA_TN = 1024


def _adaln_kernel(cond_ref, w_ref, b_ref, o_ref):
    cond = cond_ref[...]
    a = cond / (1.0 + jnp.exp(-cond))
    a_hi, a_lo = _split_bf16(a)
    w_hi, w_lo = _split_bf16(w_ref[0])
    acc = _dot(a_hi, w_hi) + (_dot(a_lo, w_hi) + _dot(a_hi, w_lo))
    o_ref[0] = acc + b_ref[0]


def _adaln(cond, w_ada, b_ada):
    n_out = 6 * D_MODEL
    return pl.pallas_call(
        _adaln_kernel,
        grid=(DEPTH, n_out // ADA_TN),
        in_specs=[
            pl.BlockSpec((COND_ROWS, D_MODEL), lambda l, n: (0, 0)),
            pl.BlockSpec((1, D_MODEL, ADA_TN), lambda l, n: (l, 0, n)),
            pl.BlockSpec((1, 1, ADA_TN), lambda l, n: (l, 0, n)),
        ],
        out_specs=pl.BlockSpec((1, COND_ROWS, ADA_TN), lambda l, n: (l, 0, n)),
        out_shape=jax.ShapeDtypeStruct((DEPTH, COND_ROWS, n_out), _F32),
        compiler_params=_params(2),
        name="adaln",
    )(cond, w_ada, b_ada.reshape(DEPTH, 1, n_out))


def _window_dr(case, i, j):
    if case == 0:
        return j - i + (MAX_WIN_H - 1) if j < MAX_WIN_H else None
    if case == 1:
        return j - i + 3 if i <= j < i + MAX_WIN_H else None
    return j - i - 1 if j >= WIN_ROWS - MAX_WIN_H else None


def _bias_kernel(rpb_ref, o_ref):
    l = pl.program_id(0)
    h = pl.program_id(1)
    cq = lax.broadcasted_iota(jnp.int32, (GRID_W, LANES), 0)
    ck = lax.broadcasted_iota(jnp.int32, (GRID_W, LANES), 1) % GRID_W
    cs = jnp.clip(cq - WIN_W // 2, 0, GRID_W - WIN_W)
    col_ok = (ck >= cs) & (ck < cs + WIN_W)
    dc = ck - cq + (WIN_W - 1)
    base = (l * N_HEADS_NA + h) * (N_DR * N_DC)
    toeplitz = []
    for dr in range(N_DR):
        t = jnp.full((GRID_W, LANES), MASK_VALUE, _F32)
        for d in range(N_DC):
            t = jnp.where(col_ok & (dc == d), rpb_ref[base + dr * N_DC + d] * LOG2_E, t)
        toeplitz.append(t)
    masked = jnp.full((GRID_W, LANES), MASK_VALUE, _F32)
    first_half = lax.broadcasted_iota(jnp.int32, (GRID_W, LANES), 1) < GRID_W
    for case in range(3):
        for i in range(ROWS_PER_TILE):
            for jj in range(WIN_ROWS // 2):
                dr0 = _window_dr(case, i, 2 * jj)
                dr1 = _window_dr(case, i, 2 * jj + 1)
                t0 = masked if dr0 is None else toeplitz[dr0]
                t1 = masked if dr1 is None else toeplitz[dr1]
                o_ref[0, case, 0, i * GRID_W:(i + 1) * GRID_W, jj * LANES:(jj + 1) * LANES] = (
                    jnp.where(first_half, t0, t1))


def _bias_tables(rpb):
    return pl.pallas_call(
        _bias_kernel,
        grid=(DEPTH, N_HEADS_NA),
        in_specs=[pl.BlockSpec(memory_space=pltpu.SMEM)],
        out_specs=pl.BlockSpec((1, 3, 1, TILE, WIN_KEYS), lambda l, h: (l, 0, h, 0, 0)),
        out_shape=jax.ShapeDtypeStruct((DEPTH, 3, N_HEADS_NA, TILE, WIN_KEYS), _F32),
        compiler_params=_params(2),
        name="bias_tables",
    )(rpb.reshape(-1))


def _stream_tile(stream_refs):
    if len(stream_refs) == 1:
        return stream_refs[0][0]
    lat_ref, ctx_ref = stream_refs
    return jnp.where(pl.program_id(1) >= N_LAT_TILES, ctx_ref[0], lat_ref[0])


def _stream_specs(n_streams):
    if n_streams == 1:
        return [pl.BlockSpec((1, TILE, D_MODEL), lambda b, t: (b, t, 0))]
    return [pl.BlockSpec((1, TILE, D_MODEL), lambda b, t: (b, jnp.minimum(t, N_LAT_TILES - 1), 0)),
            pl.BlockSpec((1, TILE, D_MODEL), lambda b, t: (b, 0, 0))]


def _inproj_kernel(n_streams, *refs):
    mod_ref, nw_ref, w_ref, gain_ref, ones_ref, cos_ref, sa_ref, sb_ref, o_ref = refs[n_streams:]
    x = _stream_tile(refs[:n_streams])
    ms = jnp.mean(x * x, axis=-1, keepdims=True)
    h = x * lax.rsqrt(ms + EPS) * nw_ref[...]
    h = h * (1.0 + mod_ref[0, 1:2, :]) + mod_ref[0, 0:1, :]
    p = _dot(h.astype(_BF16), w_ref[...])
    ones_bd = ones_ref[...]
    for blk in range(N_BLK):
        sl = slice(blk * LANES, (blk + 1) * LANES)
        y = p[:, sl]
        if blk in NORMED_BLKS:
            sq_hi, sq_lo = _split_bf16(y * y)
            ss = _dot(sq_hi, ones_bd) + _dot(sq_lo, ones_bd)
            y = y * lax.rsqrt(ss * (1.0 / HEAD_DIM) + EPS) * gain_ref[:, sl]
        if blk in ROPE_BLKS:
            y = (y * cos_ref[...]
                 + pltpu.roll(y, LANES - HEAD_DIM // 4, axis=1) * sa_ref[...]
                 + pltpu.roll(y, HEAD_DIM // 4, axis=1) * sb_ref[...])
        if blk < KB_BLK:
            o_ref[0, :, sl] = y.astype(_BF16)
        else:
            swapped = pltpu.roll(y, HEAD_DIM, axis=1)
            first_half, _ = _half_masks()
            out_blk = KB_BLK + N_KV_GQA * (blk - KB_BLK)
            for head, dup in enumerate((jnp.where(first_half, y, swapped),
                                        jnp.where(first_half, swapped, y))):
                o_ref[0, :, (out_blk + head) * LANES:(out_blk + head + 1) * LANES] = dup.astype(_BF16)


def _inproj(streams, mod_l, norm_w, w_in_bf, layer, gains, ones_bd, cos_t, sa_t, sb_t):
    n_tiles = T_ALL // TILE
    mod_row = lambda b, t: (jnp.where(t >= N_LAT_TILES, BATCH, b), 0, 0)
    return pl.pallas_call(
        functools.partial(_inproj_kernel, len(streams)),
        grid=(BATCH, n_tiles),
        in_specs=[
            *_stream_specs(len(streams)),
            pl.BlockSpec((1, 6, D_MODEL), mod_row),
            pl.BlockSpec((1, D_MODEL), lambda b, t: (0, 0)),
            pl.BlockSpec((None, D_MODEL, IN_COLS), lambda b, t: (layer, 0, 0)),
            pl.BlockSpec((1, IN_COLS), lambda b, t: (0, 0)),
            pl.BlockSpec((LANES, LANES), lambda b, t: (0, 0)),
            pl.BlockSpec((TILE, LANES), lambda b, t: (t, 0)),
            pl.BlockSpec((TILE, LANES), lambda b, t: (t, 0)),
            pl.BlockSpec((TILE, LANES), lambda b, t: (t, 0)),
        ],
        out_specs=pl.BlockSpec((1, TILE, PROJ_COLS), lambda b, t: (b, t, 0)),
        out_shape=jax.ShapeDtypeStruct((BATCH, T_ALL, PROJ_COLS), _BF16),
        compiler_params=_params(2),
        name="inproj",
    )(*streams, mod_l, norm_w, w_in_bf, gains, ones_bd, cos_t, sa_t, sb_t)


def _half_masks():
    lane = lax.broadcasted_iota(jnp.int32, (1, LANES), 1)
    return lane < HEAD_DIM, lane >= HEAD_DIM


def _pair_scores(q, parts, shift):
    n_q = q.shape[0]
    q2 = jnp.concatenate([jnp.where(keep, q, jnp.zeros_like(q)) for keep in _half_masks()], axis=0)
    scores = []
    for k, _, bias in parts:
        s = _dot_nt(q2, k)
        if bias is not None:
            s = s + bias[...].reshape(2 * n_q, -1)
        scores.append(s)
    if shift is None:
        m = functools.reduce(jnp.maximum, [jnp.max(s, axis=-1, keepdims=True) for s in scores])
    else:
        key_norm_bound, bias_max = shift
        q2f = q2.astype(_F32)
        m = jnp.sqrt(jnp.sum(q2f * q2f, axis=-1, keepdims=True)) * key_norm_bound + bias_max
    return scores, m


def _pair_finish(scores, m, parts):
    n_q = m.shape[0] // 2
    denom = 0.0
    acc = 0.0
    for s, (_, v, _) in zip(scores, parts):
        e = jnp.exp2(s - m)
        denom = denom + jnp.sum(e, axis=-1, keepdims=True)
        acc = acc + _dot(e.astype(_BF16), v)
    out = acc / denom
    first_half, _ = _half_masks()
    return jnp.where(first_half, out[:n_q], out[n_q:])


def _attend_lane_blocks(width, job, shift, o_ref, lookahead):
    pending = []
    for blk in range(width // LANES):
        sl = slice(blk * LANES, (blk + 1) * LANES)
        q, parts = job(sl, blk)
        pending.append((sl, _pair_scores(q, parts, shift) + (parts,)))
        if len(pending) > lookahead:
            done_sl, state = pending.pop(0)
            o_ref[0, :, done_sl] = _pair_finish(*state).astype(o_ref.dtype)
    for done_sl, state in pending:
        o_ref[0, :, done_sl] = _pair_finish(*state).astype(o_ref.dtype)


def _na_kernel(shift_ref, q_ref, k_ref, v_ref, kc_ref, vc_ref, bias_ref, o_ref):
    g = pl.program_id(1)
    is_latent = g < N_LAT_TILES
    bounded = shift_ref[0] > 0.0

    def latent(shift):
        first_row = jnp.clip(g * ROWS_PER_TILE - MAX_WIN_H // 2, 0, GRID_ROWS - WIN_ROWS)
        rows = pl.ds(pl.multiple_of(first_row * GRID_W, GRID_W), WIN_KEYS)
        case = jnp.where(g == 0, 0, jnp.where(g == N_LAT_TILES - 1, 2, 1))

        def job(sl, hp):
            win_part = (k_ref[0, rows, sl], v_ref[0, rows, sl], bias_ref.at[case, pl.ds(2 * hp, 2)])
            ctx_part = (kc_ref[0, :, sl], vc_ref[0, :, sl], None)
            return q_ref[0, :, sl], [win_part, ctx_part]
        _attend_lane_blocks(W_NA, job, shift, o_ref, lookahead=1)

    @pl.when(is_latent & bounded)
    def _():
        latent((shift_ref[0], shift_ref[1]))

    @pl.when(is_latent & jnp.logical_not(bounded))
    def _():
        latent(None)

    @pl.when(jnp.logical_not(is_latent))
    def _():
        def job(sl, hp):
            return q_ref[0, :, sl], [(kc_ref[0, :, sl], vc_ref[0, :, sl], None)]
        _attend_lane_blocks(W_NA, job, None, o_ref, lookahead=1)


def _na_attention(shift, p, bias, layer, n_tiles):
    ctx_tile = N_LAT_TILES
    return pl.pallas_call(
        _na_kernel,
        grid=(BATCH, n_tiles),
        in_specs=[
            pl.BlockSpec(memory_space=pltpu.SMEM),
            pl.BlockSpec((1, TILE, W_NA), lambda b, g: (b, g, QA_BLK * LANES // W_NA)),
            pl.BlockSpec((1, SEQ, W_NA), lambda b, g: (b, 0, KA_BLK * LANES // W_NA)),
            pl.BlockSpec((1, SEQ, W_NA), lambda b, g: (b, 0, VA_BLK * LANES // W_NA)),
            pl.BlockSpec((1, CTX_LEN, W_NA), lambda b, g: (b, ctx_tile, KA_BLK * LANES // W_NA)),
            pl.BlockSpec((1, CTX_LEN, W_NA), lambda b, g: (b, ctx_tile, VA_BLK * LANES // W_NA)),
            pl.BlockSpec((None, 3, N_HEADS_NA, TILE, WIN_KEYS), lambda b, g: (layer, 0, 0, 0, 0),
                         pipeline_mode=pl.Buffered(1)),
        ],
        out_specs=pl.BlockSpec((1, TILE, W_NA), lambda b, g: (b, g, 0)),
        out_shape=jax.ShapeDtypeStruct((BATCH, n_tiles * TILE, W_NA), _BF16),
        compiler_params=_params(2),
        name="na_attention",
    )(shift, p, p, p, p, p, bias)


def _gqa_kernel(shift_ref, q_ref, *refs):
    k_refs, v_refs, o_ref = refs[:N_KV_GQA], refs[N_KV_GQA:2 * N_KV_GQA], refs[-1]
    t = pl.program_id(1)
    is_latent = t < N_LAT_TILES
    bounded = shift_ref[0] > 0.0
    blocks_per_kv = W_GQA // LANES // N_KV_GQA

    def attend(rows, shift):
        def job(sl, j):
            kv = j // blocks_per_kv
            return q_ref[0, :, sl], [(k_refs[kv][0, rows, :], v_refs[kv][0, rows, :], None)]
        _attend_lane_blocks(W_GQA, job, shift, o_ref, lookahead=0)

    @pl.when(is_latent & bounded)
    def _():
        attend(slice(None), (shift_ref[0], shift_ref[1]))

    @pl.when(is_latent & jnp.logical_not(bounded))
    def _():
        attend(slice(None), None)

    @pl.when(jnp.logical_not(is_latent))
    def _():
        attend(slice(SEQ, T_ALL), None)


def _gqa_attention(shift, p, n_tiles):
    kv_spec = lambda blk: pl.BlockSpec((1, T_ALL, LANES), lambda b, t: (b, 0, blk))
    return pl.pallas_call(
        _gqa_kernel,
        grid=(BATCH, n_tiles),
        in_specs=[
            pl.BlockSpec(memory_space=pltpu.SMEM),
            pl.BlockSpec((1, TILE, W_GQA), lambda b, t: (b, t, QB_BLK * LANES // W_GQA)),
            *[kv_spec(KB_BLK + head) for head in range(N_KV_GQA)],
            *[kv_spec(KB_BLK + N_KV_GQA + head) for head in range(N_KV_GQA)],
        ],
        out_specs=pl.BlockSpec((1, TILE, W_GQA), lambda b, t: (b, t, 0)),
        out_shape=jax.ShapeDtypeStruct((BATCH, n_tiles * TILE, W_GQA), _BF16),
        compiler_params=_params(2),
        name="gqa_attention",
    )(shift, p, *([p] * (2 * N_KV_GQA)))


def _post_kernel(n_streams, *refs):
    oa_ref, ob_ref, mod_ref, wo_ref, nw_ref, wg_ref, wu_ref, wd_ref, out_ref = refs[n_streams:]
    x = _stream_tile(refs[:n_streams])
    attn = _dot(oa_ref[0], wo_ref[:W_NA, :]) + _dot(ob_ref[0], wo_ref[W_NA:, :])
    x = x + mod_ref[0, 2:3, :] * attn
    ms = jnp.mean(x * x, axis=-1, keepdims=True)
    h = x * lax.rsqrt(ms + EPS) * nw_ref[...]
    h = (h * (1.0 + mod_ref[0, 4:5, :]) + mod_ref[0, 3:4, :]).astype(_BF16)
    gate = _dot(h, wg_ref[...])
    up = _dot(h, wu_ref[...])
    act = (gate / (1.0 + jnp.exp(-gate)) * up).astype(_BF16)
    out_ref[0] = x + mod_ref[0, 5:6, :] * _dot(act, wd_ref[...])


def _post(streams, oa, ob, mod_l, wo_bf, norm_w, wg_bf, wu_bf, wd_bf, layer, n_tiles):
    mod_row = lambda b, t: (jnp.where(t >= N_LAT_TILES, BATCH, b), 0, 0)
    resident = lambda shape: pl.BlockSpec((None,) + shape, lambda b, t: (layer,) + (0,) * len(shape),
                                          pipeline_mode=pl.Buffered(1))
    return pl.pallas_call(
        functools.partial(_post_kernel, len(streams)),
        grid=(BATCH, n_tiles),
        in_specs=[
            *_stream_specs(len(streams)),
            pl.BlockSpec((1, TILE, W_NA), lambda b, t: (b, t, 0)),
            pl.BlockSpec((1, TILE, W_GQA), lambda b, t: (b, t, 0)),
            pl.BlockSpec((1, 6, D_MODEL), mod_row),
            resident((W_NA + W_GQA, D_MODEL)),
            resident((1, D_MODEL)),
            resident((D_MODEL, D_FF)),
            resident((D_MODEL, D_FF)),
            resident((D_FF, D_MODEL)),
        ],
        out_specs=pl.BlockSpec((1, TILE, D_MODEL), lambda b, t: (b, t, 0)),
        out_shape=jax.ShapeDtypeStruct((BATCH, n_tiles * TILE, D_MODEL), _F32),
        compiler_params=_params(2),
        name="post",
    )(*streams, oa, ob, mod_l, wo_bf, norm_w, wg_bf, wu_bf, wd_bf)


def _rope_tables():
    t = np.arange(SEQ)
    half = HEAD_DIM // 2
    inv_freq = np.float32(ROPE_THETA) ** (-np.arange(0, half, 2, dtype=np.float32) / np.float32(half))
    row = (t // GRID_W).astype(np.float32)[:, None] * inv_freq
    col = (t % GRID_W).astype(np.float32)[:, None] * inv_freq
    zeros = np.zeros((SEQ, HEAD_DIM // 4), np.float32)
    cos_head = np.concatenate([np.cos(row), np.cos(row), np.cos(col), np.cos(col)], axis=-1)
    sa_head = np.concatenate([-np.sin(row), zeros, -np.sin(col), zeros], axis=-1)
    sb_head = np.concatenate([zeros, np.sin(row), zeros, np.sin(col)], axis=-1)
    pad = lambda a, fill: jnp.asarray(np.concatenate(
        [np.tile(a, (1, LANES // HEAD_DIM)), np.full((CTX_LEN, LANES), fill, np.float32)],
        axis=0).astype(np.float32))
    return pad(cos_head, 1.0), pad(sa_head, 0.0), pad(sb_head, 0.0)


def _head_gains(qa, ka, qb, kb):
    scale = LOG2_E * HEAD_DIM ** -0.5
    ones = jnp.ones((W_NA,), _F32)
    return jnp.concatenate([
        jnp.tile(qa * scale, N_HEADS_NA), jnp.tile(ka, N_HEADS_NA), ones,
        jnp.tile(qb * scale, N_HEADS_GQA), jnp.tile(kb, N_KV_GQA), jnp.ones((W_KV,), _F32),
    ])[None, :]


def _softmax_shift(q_gain, k_gain, bias_max):
    q_norm_bound = LOG2_E * jnp.max(jnp.abs(q_gain)) * ROUNDING_SLACK
    key_norm_bound = HEAD_DIM ** 0.5 * jnp.max(jnp.abs(k_gain)) * ROUNDING_SLACK
    worst_gap = 2.0 * q_norm_bound * key_norm_bound + bias_max
    return jnp.stack([jnp.where(worst_gap <= SHIFT_RANGE, key_norm_bound, 0.0),
                      jnp.asarray(bias_max, _F32)]).astype(_F32)


def kernel(x, c, ctx, c_ctx, w_ada, b_ada, attn_norm, w_in, q_norm_a, k_norm_a, q_norm_b,
           k_norm_b, rpb, w_out, ffn_norm, w_gate, w_up, w_down):
    assert x.shape == (BATCH, SEQ, D_MODEL) and ctx.shape == (BATCH, CTX_LEN, D_MODEL)
    cond = jnp.concatenate(
        [c, c_ctx[None, :], jnp.zeros((COND_ROWS - BATCH - 1, D_MODEL), _F32)], axis=0)
    mod = _adaln(cond, w_ada, b_ada).reshape(DEPTH, COND_ROWS, 6, D_MODEL)
    bias = _bias_tables(rpb)
    cos_t, sa_t, sb_t = _rope_tables()
    lane = np.arange(LANES)
    ones_bd = jnp.asarray(lane[:, None] // HEAD_DIM == lane[None, :] // HEAD_DIM, _BF16)
    w_in_bf, w_out_bf = w_in.astype(_BF16), w_out.astype(_BF16)
    w_gate_bf, w_up_bf, w_down_bf = w_gate.astype(_BF16), w_up.astype(_BF16), w_down.astype(_BF16)

    streams = (x, ctx)
    for l in range(DEPTH):
        last = l == DEPTH - 1
        n_tiles = N_LAT_TILES if last else T_ALL // TILE
        gains = _head_gains(q_norm_a[l], k_norm_a[l], q_norm_b[l], k_norm_b[l])
        p = _inproj(streams, mod[l], attn_norm[l][None, :], w_in_bf, l, gains, ones_bd, cos_t, sa_t, sb_t)
        bias_max = LOG2_E * jnp.maximum(jnp.max(rpb[l]), 0.0)
        oa = _na_attention(_softmax_shift(q_norm_a[l], k_norm_a[l], bias_max), p, bias, l, n_tiles)
        ob = _gqa_attention(_softmax_shift(q_norm_b[l], k_norm_b[l], 0.0), p, n_tiles)
        xc = _post(streams, oa, ob, mod[l], w_out_bf, ffn_norm.reshape(DEPTH, 1, D_MODEL), w_gate_bf,
                   w_up_bf, w_down_bf, l, n_tiles)
        streams = (xc,)
    return xc
```

```python
import functools

import jax
import jax.numpy as jnp
import numpy as np
from jax import lax
from jax.experimental import pallas as pl
from jax.experimental.pallas import tpu as pltpu

D_MODEL = 1024
BATCH = 8
SEQ = 2048
DEPTH = 4
GRID_W = 64
CTX_LEN = 256
HEAD_DIM = 64
N_HEADS_NA = 8
N_HEADS_GQA = 8
N_KV_GQA = 2
GQA_GROUP = N_HEADS_GQA // N_KV_GQA
W_NA = N_HEADS_NA * HEAD_DIM
W_GQA = N_HEADS_GQA * HEAD_DIM
W_KV = N_KV_GQA * HEAD_DIM
IN_COLS = 3 * W_NA + W_GQA + 2 * W_KV
MAX_WIN_H = 8
WIN_W = 16
ROPE_THETA = 10000.0
D_FF = 2816
EPS = 1e-6

LANES = 128
T_ALL = SEQ + CTX_LEN
TILE = 256
STEP_ROWS = 512
N_LAT_STEPS = SEQ // STEP_ROWS
N_LAT_TILES = SEQ // TILE
ROWS_PER_TILE = TILE // GRID_W
GRID_ROWS = SEQ // GRID_W
WIN_ROWS = 12
WIN_KEYS = WIN_ROWS * GRID_W
N_DR = 2 * MAX_WIN_H - 1
N_DC = 2 * WIN_W - 1
MASK_VALUE = -1e30
LOG2_E = 1.4426950408889634
SHIFT_RANGE = 100.0
ROUNDING_SLACK = 1.01
COND_ROWS = 16

QA_BLK = 0
KA_BLK = W_NA // LANES
VA_BLK = 2 * W_NA // LANES
QB_BLK = 3 * W_NA // LANES
KB_BLK = (3 * W_NA + W_GQA) // LANES
VB_BLK = KB_BLK + 1
N_BLK = IN_COLS // LANES
PROJ_COLS = IN_COLS + 2 * W_KV
NORMED_BLKS = tuple(range(QA_BLK, VA_BLK)) + tuple(range(QB_BLK, VB_BLK))
ROPE_BLKS = tuple(range(QB_BLK, VB_BLK))

VMEM_LIMIT = 56 * 1024 * 1024

_BF16 = jnp.bfloat16
_F32 = jnp.float32


def _params(n_grid_dims):
    return pltpu.CompilerParams(
        dimension_semantics=("arbitrary",) * n_grid_dims,
        vmem_limit_bytes=VMEM_LIMIT)


def _split_bf16(a):
    hi = a.astype(_BF16)
    lo = (a - hi.astype(_F32)).astype(_BF16)
    return hi, lo


def _dot(a, b):
    return jnp.dot(a, b, preferred_element_type=_F32)


def _dot_nt(a, b):
    return lax.dot_general(a, b, (((1,), (1,)), ((), ())), preferred_element_type=_F32)


ADA_TN = 1024


def _adaln_kernel(cond_ref, w_ref, b_ref, o_ref):
    cond = cond_ref[...]
    a = cond / (1.0 + jnp.exp(-cond))
    a_hi, a_lo = _split_bf16(a)
    w_hi, w_lo = _split_bf16(w_ref[0])
    acc = _dot(a_hi, w_hi) + (_dot(a_lo, w_hi) + _dot(a_hi, w_lo))
    o_ref[0] = acc + b_ref[0]


def _adaln(cond, w_ada, b_ada):
    n_out = 6 * D_MODEL
    return pl.pallas_call(
        _adaln_kernel,
        grid=(DEPTH, n_out // ADA_TN),
        in_specs=[
            pl.BlockSpec((COND_ROWS, D_MODEL), lambda l, n: (0, 0)),
            pl.BlockSpec((1, D_MODEL, ADA_TN), lambda l, n: (l, 0, n)),
            pl.BlockSpec((1, 1, ADA_TN), lambda l, n: (l, 0, n)),
        ],
        out_specs=pl.BlockSpec((1, COND_ROWS, ADA_TN), lambda l, n: (l, 0, n)),
        out_shape=jax.ShapeDtypeStruct((DEPTH, COND_ROWS, n_out), _F32),
        compiler_params=_params(2),
        name="adaln",
    )(cond, w_ada, b_ada.reshape(DEPTH, 1, n_out))


def _window_dr(case, i, j):
    if case == 0:
        return j - i + (MAX_WIN_H - 1) if j < MAX_WIN_H else None
    if case == 1:
        return j - i + 3 if i <= j < i + MAX_WIN_H else None
    return j - i - 1 if j >= WIN_ROWS - MAX_WIN_H else None


def _bias_kernel(rpb_ref, o_ref):
    l = pl.program_id(0)
    h = pl.program_id(1)
    cq = lax.broadcasted_iota(jnp.int32, (GRID_W, LANES), 0)
    ck = lax.broadcasted_iota(jnp.int32, (GRID_W, LANES), 1) % GRID_W
    cs = jnp.clip(cq - WIN_W // 2, 0, GRID_W - WIN_W)
    col_ok = (ck >= cs) & (ck < cs + WIN_W)
    dc = ck - cq + (WIN_W - 1)
    base = (l * N_HEADS_NA + h) * (N_DR * N_DC)
    toeplitz = []
    for dr in range(N_DR):
        t = jnp.full((GRID_W, LANES), MASK_VALUE, _F32)
        for d in range(N_DC):
            t = jnp.where(col_ok & (dc == d), rpb_ref[base + dr * N_DC + d] * LOG2_E, t)
        toeplitz.append(t)
    masked = jnp.full((GRID_W, LANES), MASK_VALUE, _F32)
    first_half = lax.broadcasted_iota(jnp.int32, (GRID_W, LANES), 1) < GRID_W
    for case in range(3):
        for i in range(ROWS_PER_TILE):
            for jj in range(WIN_ROWS // 2):
                dr0 = _window_dr(case, i, 2 * jj)
                dr1 = _window_dr(case, i, 2 * jj + 1)
                t0 = masked if dr0 is None else toeplitz[dr0]
                t1 = masked if dr1 is None else toeplitz[dr1]
                o_ref[0, case, 0, i * GRID_W:(i + 1) * GRID_W, jj * LANES:(jj + 1) * LANES] = (
                    jnp.where(first_half, t0, t1))


def _bias_tables(rpb):
    return pl.pallas_call(
        _bias_kernel,
        grid=(DEPTH, N_HEADS_NA),
        in_specs=[pl.BlockSpec(memory_space=pltpu.SMEM)],
        out_specs=pl.BlockSpec((1, 3, 1, TILE, WIN_KEYS), lambda l, h: (l, 0, h, 0, 0)),
        out_shape=jax.ShapeDtypeStruct((DEPTH, 3, N_HEADS_NA, TILE, WIN_KEYS), _F32),
        compiler_params=_params(2),
        name="bias_tables",
    )(rpb.reshape(-1))


def _stream_specs(n_streams):
    if n_streams == 1:
        return [pl.BlockSpec((1, STEP_ROWS, D_MODEL), lambda b, t: (b, t, 0))]
    return [pl.BlockSpec((1, STEP_ROWS, D_MODEL), lambda b, t: (b, jnp.minimum(t, N_LAT_STEPS - 1), 0)),
            pl.BlockSpec((1, CTX_LEN, D_MODEL), lambda b, t: (b, 0, 0))]


def _for_stream_rows(stream_refs, body):
    lat_ref, ctx_ref = stream_refs[0], stream_refs[-1]
    is_latent = pl.program_id(1) < N_LAT_STEPS

    @pl.when(is_latent)
    def _():
        body(lat_ref[0], slice(None))

    @pl.when(jnp.logical_not(is_latent))
    def _():
        body(ctx_ref[0, :CTX_LEN, :], slice(0, CTX_LEN))


def _inproj_kernel(n_streams, *refs):
    mod_ref, nw_ref, w_ref, gain_ref, ones_ref, cos_ref, sa_ref, sb_ref, o_ref = refs[n_streams:]

    def project(x, rows):
        ms = jnp.mean(x * x, axis=-1, keepdims=True)
        h = x * lax.rsqrt(ms + EPS) * nw_ref[...]
        h = h * (1.0 + mod_ref[0, 1:2, :]) + mod_ref[0, 0:1, :]
        p = _dot(h.astype(_BF16), w_ref[...])
        ones_bd = ones_ref[...]
        for blk in range(N_BLK):
            sl = slice(blk * LANES, (blk + 1) * LANES)
            y = p[:, sl]
            if blk in NORMED_BLKS:
                sq_hi, sq_lo = _split_bf16(y * y)
                ss = _dot(sq_hi, ones_bd) + _dot(sq_lo, ones_bd)
                y = y * lax.rsqrt(ss * (1.0 / HEAD_DIM) + EPS) * gain_ref[:, sl]
            if blk in ROPE_BLKS:
                y = (y * cos_ref[rows, :]
                     + pltpu.roll(y, LANES - HEAD_DIM // 4, axis=1) * sa_ref[rows, :]
                     + pltpu.roll(y, HEAD_DIM // 4, axis=1) * sb_ref[rows, :])
            if blk < KB_BLK:
                o_ref[0, rows, sl] = y.astype(_BF16)
            else:
                swapped = pltpu.roll(y, HEAD_DIM, axis=1)
                first_half, _ = _half_masks()
                out_blk = KB_BLK + N_KV_GQA * (blk - KB_BLK)
                for head, dup in enumerate((jnp.where(first_half, y, swapped),
                                            jnp.where(first_half, swapped, y))):
                    o_ref[0, rows, (out_blk + head) * LANES:(out_blk + head + 1) * LANES] = (
                        dup.astype(_BF16))

    _for_stream_rows(refs[:n_streams], project)


def _inproj(streams, mod_l, norm_w, w_in_bf, layer, gains, ones_bd, cos_t, sa_t, sb_t):
    mod_row = lambda b, t: (jnp.where(t >= N_LAT_STEPS, BATCH, b), 0, 0)
    table = pl.BlockSpec((STEP_ROWS, LANES), lambda b, t: (t, 0))
    return pl.pallas_call(
        functools.partial(_inproj_kernel, len(streams)),
        grid=(BATCH, pl.cdiv(T_ALL, STEP_ROWS)),
        in_specs=[
            *_stream_specs(len(streams)),
            pl.BlockSpec((1, 6, D_MODEL), mod_row),
            pl.BlockSpec((1, D_MODEL), lambda b, t: (0, 0)),
            pl.BlockSpec((None, D_MODEL, IN_COLS), lambda b, t: (layer, 0, 0)),
            pl.BlockSpec((1, IN_COLS), lambda b, t: (0, 0)),
            pl.BlockSpec((LANES, LANES), lambda b, t: (0, 0)),
            table, table, table,
        ],
        out_specs=pl.BlockSpec((1, STEP_ROWS, PROJ_COLS), lambda b, t: (b, t, 0)),
        out_shape=jax.ShapeDtypeStruct((BATCH, T_ALL, PROJ_COLS), _BF16),
        compiler_params=_params(2),
        name="inproj",
    )(*streams, mod_l, norm_w, w_in_bf, gains, ones_bd, cos_t, sa_t, sb_t)


def _half_masks():
    lane = lax.broadcasted_iota(jnp.int32, (1, LANES), 1)
    return lane < HEAD_DIM, lane >= HEAD_DIM


def _pair_scores(q, parts, shift):
    n_q = q.shape[0]
    q2 = jnp.concatenate([jnp.where(keep, q, jnp.zeros_like(q)) for keep in _half_masks()], axis=0)
    scores = []
    for k, _, bias in parts:
        s = _dot_nt(q2, k)
        if bias is not None:
            s = s + bias[...].reshape(2 * n_q, -1)
        scores.append(s)
    if shift is None:
        m = functools.reduce(jnp.maximum, [jnp.max(s, axis=-1, keepdims=True) for s in scores])
    else:
        key_norm_bound, bias_max = shift
        q2f = q2.astype(_F32)
        m = jnp.sqrt(jnp.sum(q2f * q2f, axis=-1, keepdims=True)) * key_norm_bound + bias_max
    return scores, m


def _pair_finish(scores, m, parts):
    n_q = m.shape[0] // 2
    denom = 0.0
    acc = 0.0
    for s, (_, v, _) in zip(scores, parts):
        e = jnp.exp2(s - m)
        denom = denom + jnp.sum(e, axis=-1, keepdims=True)
        acc = acc + _dot(e.astype(_BF16), v)
    out = acc / denom
    first_half, _ = _half_masks()
    return jnp.where(first_half, out[:n_q], out[n_q:])


def _attend_lane_blocks(width, job, shift, o_ref, lookahead):
    pending = []
    for blk in range(width // LANES):
        sl = slice(blk * LANES, (blk + 1) * LANES)
        q, parts = job(sl, blk)
        pending.append((sl, _pair_scores(q, parts, shift) + (parts,)))
        if len(pending) > lookahead:
            done_sl, state = pending.pop(0)
            o_ref[0, :, done_sl] = _pair_finish(*state).astype(o_ref.dtype)
    for done_sl, state in pending:
        o_ref[0, :, done_sl] = _pair_finish(*state).astype(o_ref.dtype)


def _na_kernel(shift_ref, q_ref, k_ref, v_ref, kc_ref, vc_ref, bias_ref, o_ref):
    g = pl.program_id(1)
    is_latent = g < N_LAT_TILES
    bounded = shift_ref[0] > 0.0

    def latent(shift):
        first_row = jnp.clip(g * ROWS_PER_TILE - MAX_WIN_H // 2, 0, GRID_ROWS - WIN_ROWS)
        rows = pl.ds(pl.multiple_of(first_row * GRID_W, GRID_W), WIN_KEYS)
        case = jnp.where(g == 0, 0, jnp.where(g == N_LAT_TILES - 1, 2, 1))

        def job(sl, hp):
            win_part = (k_ref[0, rows, sl], v_ref[0, rows, sl], bias_ref.at[case, pl.ds(2 * hp, 2)])
            ctx_part = (kc_ref[0, :, sl], vc_ref[0, :, sl], None)
            return q_ref[0, :, sl], [win_part, ctx_part]
        _attend_lane_blocks(W_NA, job, shift, o_ref, lookahead=1)

    @pl.when(is_latent & bounded)
    def _():
        latent((shift_ref[0], shift_ref[1]))

    @pl.when(is_latent & jnp.logical_not(bounded))
    def _():
        latent(None)

    @pl.when(jnp.logical_not(is_latent))
    def _():
        def job(sl, hp):
            return q_ref[0, :, sl], [(kc_ref[0, :, sl], vc_ref[0, :, sl], None)]
        _attend_lane_blocks(W_NA, job, None, o_ref, lookahead=1)


def _na_attention(shift, p, bias, layer, n_tiles):
    ctx_tile = N_LAT_TILES
    return pl.pallas_call(
        _na_kernel,
        grid=(BATCH, n_tiles),
        in_specs=[
            pl.BlockSpec(memory_space=pltpu.SMEM),
            pl.BlockSpec((1, TILE, W_NA), lambda b, g: (b, g, QA_BLK * LANES // W_NA)),
            pl.BlockSpec((1, SEQ, W_NA), lambda b, g: (b, 0, KA_BLK * LANES // W_NA)),
            pl.BlockSpec((1, SEQ, W_NA), lambda b, g: (b, 0, VA_BLK * LANES // W_NA)),
            pl.BlockSpec((1, CTX_LEN, W_NA), lambda b, g: (b, ctx_tile, KA_BLK * LANES // W_NA)),
            pl.BlockSpec((1, CTX_LEN, W_NA), lambda b, g: (b, ctx_tile, VA_BLK * LANES // W_NA)),
            pl.BlockSpec((None, 3, N_HEADS_NA, TILE, WIN_KEYS), lambda b, g: (layer, 0, 0, 0, 0),
                         pipeline_mode=pl.Buffered(1)),
        ],
        out_specs=pl.BlockSpec((1, TILE, W_NA), lambda b, g: (b, g, 0)),
        out_shape=jax.ShapeDtypeStruct((BATCH, n_tiles * TILE, W_NA), _BF16),
        compiler_params=_params(2),
        name="na_attention",
    )(shift, p, p, p, p, p, bias)


def _gqa_kernel(shift_ref, q_ref, *refs):
    k_refs, v_refs, o_ref = refs[:N_KV_GQA], refs[N_KV_GQA:2 * N_KV_GQA], refs[-1]
    t = pl.program_id(1)
    is_latent = t < N_LAT_TILES
    bounded = shift_ref[0] > 0.0
    blocks_per_kv = W_GQA // LANES // N_KV_GQA

    def attend(rows, shift):
        def job(sl, j):
            kv = j // blocks_per_kv
            return q_ref[0, :, sl], [(k_refs[kv][0, rows, :], v_refs[kv][0, rows, :], None)]
        _attend_lane_blocks(W_GQA, job, shift, o_ref, lookahead=0)

    @pl.when(is_latent & bounded)
    def _():
        attend(slice(None), (shift_ref[0], shift_ref[1]))

    @pl.when(is_latent & jnp.logical_not(bounded))
    def _():
        attend(slice(None), None)

    @pl.when(jnp.logical_not(is_latent))
    def _():
        attend(slice(SEQ, T_ALL), None)


def _gqa_attention(shift, p, n_tiles):
    kv_spec = lambda blk: pl.BlockSpec((1, T_ALL, LANES), lambda b, t: (b, 0, blk))
    return pl.pallas_call(
        _gqa_kernel,
        grid=(BATCH, n_tiles),
        in_specs=[
            pl.BlockSpec(memory_space=pltpu.SMEM),
            pl.BlockSpec((1, TILE, W_GQA), lambda b, t: (b, t, QB_BLK * LANES // W_GQA)),
            *[kv_spec(KB_BLK + head) for head in range(N_KV_GQA)],
            *[kv_spec(KB_BLK + N_KV_GQA + head) for head in range(N_KV_GQA)],
        ],
        out_specs=pl.BlockSpec((1, TILE, W_GQA), lambda b, t: (b, t, 0)),
        out_shape=jax.ShapeDtypeStruct((BATCH, n_tiles * TILE, W_GQA), _BF16),
        compiler_params=_params(2),
        name="gqa_attention",
    )(shift, p, *([p] * (2 * N_KV_GQA)))


def _post_kernel(n_streams, *refs):
    oa_ref, ob_ref, mod_ref, wo_ref, nw_ref, wg_ref, wu_ref, wd_ref, out_ref = refs[n_streams:]

    def update(x, rows):
        attn = _dot(oa_ref[0, rows, :], wo_ref[:W_NA, :]) + _dot(ob_ref[0, rows, :], wo_ref[W_NA:, :])
        x = x + mod_ref[0, 2:3, :] * attn
        ms = jnp.mean(x * x, axis=-1, keepdims=True)
        h = x * lax.rsqrt(ms + EPS) * nw_ref[...]
        h = (h * (1.0 + mod_ref[0, 4:5, :]) + mod_ref[0, 3:4, :]).astype(_BF16)
        gate = _dot(h, wg_ref[...])
        up = _dot(h, wu_ref[...])
        act = (gate / (1.0 + jnp.exp(-gate)) * up).astype(_BF16)
        out_ref[0, rows, :] = x + mod_ref[0, 5:6, :] * _dot(act, wd_ref[...])

    _for_stream_rows(refs[:n_streams], update)


def _post(streams, oa, ob, mod_l, wo_bf, norm_w, wg_bf, wu_bf, wd_bf, layer, n_tiles):
    n_rows = n_tiles * TILE
    mod_row = lambda b, t: (jnp.where(t >= N_LAT_STEPS, BATCH, b), 0, 0)
    resident = lambda shape: pl.BlockSpec((None,) + shape, lambda b, t: (layer,) + (0,) * len(shape),
                                          pipeline_mode=pl.Buffered(1))
    rows_of = lambda width: pl.BlockSpec((1, STEP_ROWS, width), lambda b, t: (b, t, 0))
    return pl.pallas_call(
        functools.partial(_post_kernel, len(streams)),
        grid=(BATCH, pl.cdiv(n_rows, STEP_ROWS)),
        in_specs=[
            *_stream_specs(len(streams)),
            rows_of(W_NA),
            rows_of(W_GQA),
            pl.BlockSpec((1, 6, D_MODEL), mod_row),
            resident((W_NA + W_GQA, D_MODEL)),
            resident((1, D_MODEL)),
            resident((D_MODEL, D_FF)),
            resident((D_MODEL, D_FF)),
            resident((D_FF, D_MODEL)),
        ],
        out_specs=rows_of(D_MODEL),
        out_shape=jax.ShapeDtypeStruct((BATCH, n_rows, D_MODEL), _F32),
        compiler_params=_params(2),
        name="post",
    )(*streams, oa, ob, mod_l, wo_bf, norm_w, wg_bf, wu_bf, wd_bf)


def _rope_tables():
    t = np.arange(SEQ)
    half = HEAD_DIM // 2
    inv_freq = np.float32(ROPE_THETA) ** (-np.arange(0, half, 2, dtype=np.float32) / np.float32(half))
    row = (t // GRID_W).astype(np.float32)[:, None] * inv_freq
    col = (t % GRID_W).astype(np.float32)[:, None] * inv_freq
    zeros = np.zeros((SEQ, HEAD_DIM // 4), np.float32)
    cos_head = np.concatenate([np.cos(row), np.cos(row), np.cos(col), np.cos(col)], axis=-1)
    sa_head = np.concatenate([-np.sin(row), zeros, -np.sin(col), zeros], axis=-1)
    sb_head = np.concatenate([zeros, np.sin(row), zeros, np.sin(col)], axis=-1)
    pad = lambda a, fill: jnp.asarray(np.concatenate(
        [np.tile(a, (1, LANES // HEAD_DIM)), np.full((CTX_LEN, LANES), fill, np.float32)],
        axis=0).astype(np.float32))
    return pad(cos_head, 1.0), pad(sa_head, 0.0), pad(sb_head, 0.0)


def _head_gains(qa, ka, qb, kb):
    scale = LOG2_E * HEAD_DIM ** -0.5
    ones = jnp.ones((W_NA,), _F32)
    return jnp.concatenate([
        jnp.tile(qa * scale, N_HEADS_NA), jnp.tile(ka, N_HEADS_NA), ones,
        jnp.tile(qb * scale, N_HEADS_GQA), jnp.tile(kb, N_KV_GQA), jnp.ones((W_KV,), _F32),
    ])[None, :]


def _softmax_shift(q_gain, k_gain, bias_max):
    q_norm_bound = LOG2_E * jnp.max(jnp.abs(q_gain)) * ROUNDING_SLACK
    key_norm_bound = HEAD_DIM ** 0.5 * jnp.max(jnp.abs(k_gain)) * ROUNDING_SLACK
    worst_gap = 2.0 * q_norm_bound * key_norm_bound + bias_max
    return jnp.stack([jnp.where(worst_gap <= SHIFT_RANGE, key_norm_bound, 0.0),
                      jnp.asarray(bias_max, _F32)]).astype(_F32)


def kernel(x, c, ctx, c_ctx, w_ada, b_ada, attn_norm, w_in, q_norm_a, k_norm_a, q_norm_b,
           k_norm_b, rpb, w_out, ffn_norm, w_gate, w_up, w_down):
    assert x.shape == (BATCH, SEQ, D_MODEL) and ctx.shape == (BATCH, CTX_LEN, D_MODEL)
    cond = jnp.concatenate(
        [c, c_ctx[None, :], jnp.zeros((COND_ROWS - BATCH - 1, D_MODEL), _F32)], axis=0)
    mod = _adaln(cond, w_ada, b_ada).reshape(DEPTH, COND_ROWS, 6, D_MODEL)
    bias = _bias_tables(rpb)
    cos_t, sa_t, sb_t = _rope_tables()
    lane = np.arange(LANES)
    ones_bd = jnp.asarray(lane[:, None] // HEAD_DIM == lane[None, :] // HEAD_DIM, _BF16)
    w_in_bf, w_out_bf = w_in.astype(_BF16), w_out.astype(_BF16)
    w_gate_bf, w_up_bf, w_down_bf = w_gate.astype(_BF16), w_up.astype(_BF16), w_down.astype(_BF16)

    streams = (x, ctx)
    for l in range(DEPTH):
        last = l == DEPTH - 1
        n_tiles = N_LAT_TILES if last else T_ALL // TILE
        gains = _head_gains(q_norm_a[l], k_norm_a[l], q_norm_b[l], k_norm_b[l])
        p = _inproj(streams, mod[l], attn_norm[l][None, :], w_in_bf, l, gains, ones_bd, cos_t, sa_t, sb_t)
        bias_max = LOG2_E * jnp.maximum(jnp.max(rpb[l]), 0.0)
        oa = _na_attention(_softmax_shift(q_norm_a[l], k_norm_a[l], bias_max), p, bias, l, n_tiles)
        ob = _gqa_attention(_softmax_shift(q_norm_b[l], k_norm_b[l], 0.0), p, n_tiles)
        xc = _post(streams, oa, ob, mod[l], w_out_bf, ffn_norm.reshape(DEPTH, 1, D_MODEL), w_gate_bf,
                   w_up_bf, w_down_bf, l, n_tiles)
        streams = (xc,)
    return xc
```

```python
import functools

import jax
import jax.numpy as jnp
import numpy as np
from jax import lax
from jax.experimental import pallas as pl
from jax.experimental.pallas import tpu as pltpu

D_MODEL = 1024
BATCH = 8
SEQ = 2048
DEPTH = 4
GRID_W = 64
CTX_LEN = 256
HEAD_DIM = 64
N_HEADS_NA = 8
N_HEADS_GQA = 8
N_KV_GQA = 2
GQA_GROUP = N_HEADS_GQA // N_KV_GQA
W_NA = N_HEADS_NA * HEAD_DIM
W_GQA = N_HEADS_GQA * HEAD_DIM
W_KV = N_KV_GQA * HEAD_DIM
IN_COLS = 3 * W_NA + W_GQA + 2 * W_KV
MAX_WIN_H = 8
WIN_W = 16
ROPE_THETA = 10000.0
D_FF = 2816
EPS = 1e-6

LANES = 128
T_ALL = SEQ + CTX_LEN
TILE = 256
STEP_ROWS = 512
N_LAT_STEPS = SEQ // STEP_ROWS
N_LAT_TILES = SEQ // TILE
ROWS_PER_TILE = TILE // GRID_W
GRID_ROWS = SEQ // GRID_W
WIN_ROWS = 12
WIN_KEYS = WIN_ROWS * GRID_W
N_DR = 2 * MAX_WIN_H - 1
N_DC = 2 * WIN_W - 1
MASK_VALUE = -1e30
LOG2_E = 1.4426950408889634
SHIFT_RANGE = 100.0
ROUNDING_SLACK = 1.01
COND_ROWS = 16

QA_BLK = 0
KA_BLK = W_NA // LANES
VA_BLK = 2 * W_NA // LANES
QB_BLK = 3 * W_NA // LANES
KB_BLK = (3 * W_NA + W_GQA) // LANES
VB_BLK = KB_BLK + 1
N_BLK = IN_COLS // LANES
PROJ_COLS = IN_COLS + 2 * W_KV
NORMED_BLKS = tuple(range(QA_BLK, VA_BLK)) + tuple(range(QB_BLK, VB_BLK))
ROPE_BLKS = tuple(range(QB_BLK, VB_BLK))

VMEM_LIMIT = 56 * 1024 * 1024

_BF16 = jnp.bfloat16
_F32 = jnp.float32


def _params(n_grid_dims):
    return pltpu.CompilerParams(
        dimension_semantics=("arbitrary",) * n_grid_dims,
        vmem_limit_bytes=VMEM_LIMIT)


def _split_bf16(a):
    hi = a.astype(_BF16)
    lo = (a - hi.astype(_F32)).astype(_BF16)
    return hi, lo


def _dot(a, b):
    return jnp.dot(a, b, preferred_element_type=_F32)


def _dot_nt(a, b):
    return lax.dot_general(a, b, (((1,), (1,)), ((), ())), preferred_element_type=_F32)


ADA_TN = 1024


def _adaln_kernel(cond_ref, w_ref, b_ref, o_ref):
    cond = cond_ref[...]
    a = cond / (1.0 + jnp.exp(-cond))
    a_hi, a_lo = _split_bf16(a)
    w_hi, w_lo = _split_bf16(w_ref[0])
    acc = _dot(a_hi, w_hi) + (_dot(a_lo, w_hi) + _dot(a_hi, w_lo))
    o_ref[0] = acc + b_ref[0]


def _adaln(cond, w_ada, b_ada):
    n_out = 6 * D_MODEL
    return pl.pallas_call(
        _adaln_kernel,
        grid=(DEPTH, n_out // ADA_TN),
        in_specs=[
            pl.BlockSpec((COND_ROWS, D_MODEL), lambda l, n: (0, 0)),
            pl.BlockSpec((1, D_MODEL, ADA_TN), lambda l, n: (l, 0, n)),
            pl.BlockSpec((1, 1, ADA_TN), lambda l, n: (l, 0, n)),
        ],
        out_specs=pl.BlockSpec((1, COND_ROWS, ADA_TN), lambda l, n: (l, 0, n)),
        out_shape=jax.ShapeDtypeStruct((DEPTH, COND_ROWS, n_out), _F32),
        compiler_params=_params(2),
        name="adaln",
    )(cond, w_ada, b_ada.reshape(DEPTH, 1, n_out))


def _window_dr(case, i, j):
    if case == 0:
        return j - i + (MAX_WIN_H - 1) if j < MAX_WIN_H else None
    if case == 1:
        return j - i + 3 if i <= j < i + MAX_WIN_H else None
    return j - i - 1 if j >= WIN_ROWS - MAX_WIN_H else None


def _bias_kernel(rpb_ref, o_ref):
    l = pl.program_id(0)
    h = pl.program_id(1)
    cq = lax.broadcasted_iota(jnp.int32, (GRID_W, LANES), 0)
    ck = lax.broadcasted_iota(jnp.int32, (GRID_W, LANES), 1) % GRID_W
    cs = jnp.clip(cq - WIN_W // 2, 0, GRID_W - WIN_W)
    col_ok = (ck >= cs) & (ck < cs + WIN_W)
    dc = ck - cq + (WIN_W - 1)
    base = (l * N_HEADS_NA + h) * (N_DR * N_DC)
    toeplitz = []
    for dr in range(N_DR):
        t = jnp.full((GRID_W, LANES), MASK_VALUE, _F32)
        for d in range(N_DC):
            t = jnp.where(col_ok & (dc == d), rpb_ref[base + dr * N_DC + d] * LOG2_E, t)
        toeplitz.append(t)
    masked = jnp.full((GRID_W, LANES), MASK_VALUE, _F32)
    first_half = lax.broadcasted_iota(jnp.int32, (GRID_W, LANES), 1) < GRID_W
    for case in range(3):
        for i in range(ROWS_PER_TILE):
            for jj in range(WIN_ROWS // 2):
                dr0 = _window_dr(case, i, 2 * jj)
                dr1 = _window_dr(case, i, 2 * jj + 1)
                t0 = masked if dr0 is None else toeplitz[dr0]
                t1 = masked if dr1 is None else toeplitz[dr1]
                o_ref[0, case, 0, i * GRID_W:(i + 1) * GRID_W, jj * LANES:(jj + 1) * LANES] = (
                    jnp.where(first_half, t0, t1))


def _bias_tables(rpb):
    return pl.pallas_call(
        _bias_kernel,
        grid=(DEPTH, N_HEADS_NA),
        in_specs=[pl.BlockSpec(memory_space=pltpu.SMEM)],
        out_specs=pl.BlockSpec((1, 3, 1, TILE, WIN_KEYS), lambda l, h: (l, 0, h, 0, 0)),
        out_shape=jax.ShapeDtypeStruct((DEPTH, 3, N_HEADS_NA, TILE, WIN_KEYS), _F32),
        compiler_params=_params(2),
        name="bias_tables",
    )(rpb.reshape(-1))


def _stream_specs(n_streams):
    if n_streams == 1:
        return [pl.BlockSpec((1, STEP_ROWS, D_MODEL), lambda b, t: (b, t, 0))]
    return [pl.BlockSpec((1, STEP_ROWS, D_MODEL), lambda b, t: (b, jnp.minimum(t, N_LAT_STEPS - 1), 0)),
            pl.BlockSpec((1, CTX_LEN, D_MODEL), lambda b, t: (b, 0, 0))]


def _for_stream_rows(stream_refs, body):
    lat_ref, ctx_ref = stream_refs[0], stream_refs[-1]
    is_latent = pl.program_id(1) < N_LAT_STEPS

    @pl.when(is_latent)
    def _():
        body(lat_ref[0], slice(None))

    @pl.when(jnp.logical_not(is_latent))
    def _():
        body(ctx_ref[0, :CTX_LEN, :], slice(0, CTX_LEN))


def _inproj_kernel(n_streams, *refs):
    mod_ref, nw_ref, w_ref, gain_ref, ones_ref, cos_ref, sa_ref, sb_ref, o_ref = refs[n_streams:]

    def project(x, rows):
        ms = jnp.mean(x * x, axis=-1, keepdims=True)
        h = x * lax.rsqrt(ms + EPS) * nw_ref[...]
        h = h * (1.0 + mod_ref[0, 1:2, :]) + mod_ref[0, 0:1, :]
        p = _dot(h.astype(_BF16), w_ref[...])
        ones_bd = ones_ref[...]
        for blk in range(N_BLK):
            sl = slice(blk * LANES, (blk + 1) * LANES)
            y = p[:, sl]
            if blk in NORMED_BLKS:
                sq_hi, sq_lo = _split_bf16(y * y)
                ss = _dot(sq_hi, ones_bd) + _dot(sq_lo, ones_bd)
                y = y * lax.rsqrt(ss * (1.0 / HEAD_DIM) + EPS) * gain_ref[:, sl]
            if blk in ROPE_BLKS:
                y = (y * cos_ref[rows, :]
                     + pltpu.roll(y, LANES - HEAD_DIM // 4, axis=1) * sa_ref[rows, :]
                     + pltpu.roll(y, HEAD_DIM // 4, axis=1) * sb_ref[rows, :])
            if blk < KB_BLK:
                o_ref[0, rows, sl] = y.astype(_BF16)
            else:
                swapped = pltpu.roll(y, HEAD_DIM, axis=1)
                first_half, _ = _half_masks()
                out_blk = KB_BLK + N_KV_GQA * (blk - KB_BLK)
                for head, dup in enumerate((jnp.where(first_half, y, swapped),
                                            jnp.where(first_half, swapped, y))):
                    o_ref[0, rows, (out_blk + head) * LANES:(out_blk + head + 1) * LANES] = (
                        dup.astype(_BF16))

    _for_stream_rows(refs[:n_streams], project)


def _inproj(streams, mod_l, norm_w, w_in_bf, layer, gains, ones_bd, cos_t, sa_t, sb_t):
    mod_row = lambda b, t: (jnp.where(t >= N_LAT_STEPS, BATCH, b), 0, 0)
    table = pl.BlockSpec((STEP_ROWS, LANES), lambda b, t: (t, 0))
    return pl.pallas_call(
        functools.partial(_inproj_kernel, len(streams)),
        grid=(BATCH, pl.cdiv(T_ALL, STEP_ROWS)),
        in_specs=[
            *_stream_specs(len(streams)),
            pl.BlockSpec((1, 6, D_MODEL), mod_row),
            pl.BlockSpec((1, D_MODEL), lambda b, t: (0, 0)),
            pl.BlockSpec((None, D_MODEL, IN_COLS), lambda b, t: (layer, 0, 0)),
            pl.BlockSpec((1, IN_COLS), lambda b, t: (0, 0)),
            pl.BlockSpec((LANES, LANES), lambda b, t: (0, 0)),
            table, table, table,
        ],
        out_specs=pl.BlockSpec((1, STEP_ROWS, PROJ_COLS), lambda b, t: (b, t, 0)),
        out_shape=jax.ShapeDtypeStruct((BATCH, T_ALL, PROJ_COLS), _BF16),
        compiler_params=_params(2),
        name="inproj",
    )(*streams, mod_l, norm_w, w_in_bf, gains, ones_bd, cos_t, sa_t, sb_t)


def _half_masks():
    lane = lax.broadcasted_iota(jnp.int32, (1, LANES), 1)
    return lane < HEAD_DIM, lane >= HEAD_DIM


def _pair_scores(q, parts, shift):
    n_q = q.shape[0]
    q2 = jnp.concatenate([jnp.where(keep, q, jnp.zeros_like(q)) for keep in _half_masks()], axis=0)
    scores = []
    for k, _, bias in parts:
        s = _dot_nt(q2, k)
        if bias is not None:
            s = s + bias[...].reshape(2 * n_q, -1)
        scores.append(s)
    if shift is None:
        m = functools.reduce(jnp.maximum, [jnp.max(s, axis=-1, keepdims=True) for s in scores])
    else:
        key_norm_bound, bias_max = shift
        q2f = q2.astype(_F32)
        m = jnp.sqrt(jnp.sum(q2f * q2f, axis=-1, keepdims=True)) * key_norm_bound + bias_max
    return scores, m


def _pair_finish(scores, m, parts):
    n_q = m.shape[0] // 2
    denom = 0.0
    acc = 0.0
    for s, (_, v, _) in zip(scores, parts):
        e = jnp.exp2(s - m)
        denom = denom + jnp.sum(e, axis=-1, keepdims=True)
        acc = acc + _dot(e.astype(_BF16), v)
    out = acc / denom
    first_half, _ = _half_masks()
    return jnp.where(first_half, out[:n_q], out[n_q:])


def _attend_jobs(jobs, shift, o_ref, lookahead):
    pending = []

    def finish(dest, state):
        o_ref[(0,) + dest] = _pair_finish(*state).astype(o_ref.dtype)

    for dest, thunk in jobs:
        q, parts = thunk()
        pending.append((dest, _pair_scores(q, parts, shift) + (parts,)))
        if len(pending) > lookahead:
            finish(*pending.pop(0))
    for item in pending:
        finish(*item)


def _lane_blocks(width):
    return [(blk, slice(blk * LANES, (blk + 1) * LANES)) for blk in range(width // LANES)]


def _na_kernel(shift_ref, q_ref, k_ref, v_ref, kc_ref, vc_ref, bias_ref, o_ref):
    step = pl.program_id(1)
    is_latent = step < N_LAT_STEPS
    bounded = shift_ref[0] > 0.0

    def ctx_part(sl):
        return (kc_ref[0, :, sl], vc_ref[0, :, sl], None)

    def latent(shift):
        jobs = []
        for sub in range(STEP_ROWS // TILE):
            g = step * (STEP_ROWS // TILE) + sub
            first_row = jnp.clip(g * ROWS_PER_TILE - MAX_WIN_H // 2, 0, GRID_ROWS - WIN_ROWS)
            rows = pl.ds(pl.multiple_of(first_row * GRID_W, GRID_W), WIN_KEYS)
            case = jnp.where(g == 0, 0, jnp.where(g == N_LAT_TILES - 1, 2, 1))
            q_rows = slice(sub * TILE, (sub + 1) * TILE)
            for hp, sl in _lane_blocks(W_NA):
                def thunk(hp=hp, sl=sl, rows=rows, case=case, q_rows=q_rows):
                    win_part = (k_ref[0, rows, sl], v_ref[0, rows, sl],
                                bias_ref.at[case, pl.ds(2 * hp, 2)])
                    return q_ref[0, q_rows, sl], [win_part, ctx_part(sl)]
                jobs.append(((q_rows, sl), thunk))
        _attend_jobs(jobs, shift, o_ref, lookahead=1)

    @pl.when(is_latent & bounded)
    def _():
        latent((shift_ref[0], shift_ref[1]))

    @pl.when(is_latent & jnp.logical_not(bounded))
    def _():
        latent(None)

    @pl.when(jnp.logical_not(is_latent))
    def _():
        q_rows = slice(0, CTX_LEN)
        jobs = [((q_rows, sl), lambda sl=sl: (q_ref[0, q_rows, sl], [ctx_part(sl)]))
                for _, sl in _lane_blocks(W_NA)]
        _attend_jobs(jobs, None, o_ref, lookahead=1)


def _na_attention(shift, p, bias, layer, n_tiles):
    ctx_tile = N_LAT_TILES
    return pl.pallas_call(
        _na_kernel,
        grid=(BATCH, pl.cdiv(n_tiles * TILE, STEP_ROWS)),
        in_specs=[
            pl.BlockSpec(memory_space=pltpu.SMEM),
            pl.BlockSpec((1, STEP_ROWS, W_NA), lambda b, g: (b, g, QA_BLK * LANES // W_NA)),
            pl.BlockSpec((1, SEQ, W_NA), lambda b, g: (b, 0, KA_BLK * LANES // W_NA)),
            pl.BlockSpec((1, SEQ, W_NA), lambda b, g: (b, 0, VA_BLK * LANES // W_NA)),
            pl.BlockSpec((1, CTX_LEN, W_NA), lambda b, g: (b, ctx_tile, KA_BLK * LANES // W_NA)),
            pl.BlockSpec((1, CTX_LEN, W_NA), lambda b, g: (b, ctx_tile, VA_BLK * LANES // W_NA)),
            pl.BlockSpec((None, 3, N_HEADS_NA, TILE, WIN_KEYS), lambda b, g: (layer, 0, 0, 0, 0),
                         pipeline_mode=pl.Buffered(1)),
        ],
        out_specs=pl.BlockSpec((1, STEP_ROWS, W_NA), lambda b, g: (b, g, 0)),
        out_shape=jax.ShapeDtypeStruct((BATCH, n_tiles * TILE, W_NA), _BF16),
        compiler_params=_params(2),
        name="na_attention",
    )(shift, p, p, p, p, p, bias)


def _gqa_kernel(shift_ref, q_ref, *refs):
    k_refs, v_refs, o_ref = refs[:N_KV_GQA], refs[N_KV_GQA:2 * N_KV_GQA], refs[-1]
    is_latent = pl.program_id(1) < N_LAT_STEPS
    bounded = shift_ref[0] > 0.0
    blocks_per_kv = W_GQA // LANES // N_KV_GQA

    def attend(q_rows, kv_rows, shift):
        def thunk(j, sl):
            kv = j // blocks_per_kv
            return q_ref[0, q_rows, sl], [(k_refs[kv][0, kv_rows, :], v_refs[kv][0, kv_rows, :], None)]
        jobs = [((q_rows, sl), functools.partial(thunk, j, sl)) for j, sl in _lane_blocks(W_GQA)]
        _attend_jobs(jobs, shift, o_ref, lookahead=0)

    @pl.when(is_latent & bounded)
    def _():
        attend(slice(None), slice(None), (shift_ref[0], shift_ref[1]))

    @pl.when(is_latent & jnp.logical_not(bounded))
    def _():
        attend(slice(None), slice(None), None)

    @pl.when(jnp.logical_not(is_latent))
    def _():
        attend(slice(0, CTX_LEN), slice(SEQ, T_ALL), None)


def _gqa_attention(shift, p, n_tiles):
    kv_spec = lambda blk: pl.BlockSpec((1, T_ALL, LANES), lambda b, t: (b, 0, blk))
    return pl.pallas_call(
        _gqa_kernel,
        grid=(BATCH, pl.cdiv(n_tiles * TILE, STEP_ROWS)),
        in_specs=[
            pl.BlockSpec(memory_space=pltpu.SMEM),
            pl.BlockSpec((1, STEP_ROWS, W_GQA), lambda b, t: (b, t, QB_BLK * LANES // W_GQA)),
            *[kv_spec(KB_BLK + head) for head in range(N_KV_GQA)],
            *[kv_spec(KB_BLK + N_KV_GQA + head) for head in range(N_KV_GQA)],
        ],
        out_specs=pl.BlockSpec((1, STEP_ROWS, W_GQA), lambda b, t: (b, t, 0)),
        out_shape=jax.ShapeDtypeStruct((BATCH, n_tiles * TILE, W_GQA), _BF16),
        compiler_params=_params(2),
        name="gqa_attention",
    )(shift, p, *([p] * (2 * N_KV_GQA)))


def _post_kernel(n_streams, *refs):
    oa_ref, ob_ref, mod_ref, wo_ref, nw_ref, wg_ref, wu_ref, wd_ref, out_ref = refs[n_streams:]

    def update(x, rows):
        attn = _dot(oa_ref[0, rows, :], wo_ref[:W_NA, :]) + _dot(ob_ref[0, rows, :], wo_ref[W_NA:, :])
        x = x + mod_ref[0, 2:3, :] * attn
        ms = jnp.mean(x * x, axis=-1, keepdims=True)
        h = x * lax.rsqrt(ms + EPS) * nw_ref[...]
        h = (h * (1.0 + mod_ref[0, 4:5, :]) + mod_ref[0, 3:4, :]).astype(_BF16)
        gate = _dot(h, wg_ref[...])
        up = _dot(h, wu_ref[...])
        act = (gate / (1.0 + jnp.exp(-gate)) * up).astype(_BF16)
        out_ref[0, rows, :] = x + mod_ref[0, 5:6, :] * _dot(act, wd_ref[...])

    _for_stream_rows(refs[:n_streams], update)


def _post(streams, oa, ob, mod_l, wo_bf, norm_w, wg_bf, wu_bf, wd_bf, layer, n_tiles):
    n_rows = n_tiles * TILE
    mod_row = lambda b, t: (jnp.where(t >= N_LAT_STEPS, BATCH, b), 0, 0)
    resident = lambda shape: pl.BlockSpec((None,) + shape, lambda b, t: (layer,) + (0,) * len(shape),
                                          pipeline_mode=pl.Buffered(1))
    rows_of = lambda width: pl.BlockSpec((1, STEP_ROWS, width), lambda b, t: (b, t, 0))
    return pl.pallas_call(
        functools.partial(_post_kernel, len(streams)),
        grid=(BATCH, pl.cdiv(n_rows, STEP_ROWS)),
        in_specs=[
            *_stream_specs(len(streams)),
            rows_of(W_NA),
            rows_of(W_GQA),
            pl.BlockSpec((1, 6, D_MODEL), mod_row),
            resident((W_NA + W_GQA, D_MODEL)),
            resident((1, D_MODEL)),
            resident((D_MODEL, D_FF)),
            resident((D_MODEL, D_FF)),
            resident((D_FF, D_MODEL)),
        ],
        out_specs=rows_of(D_MODEL),
        out_shape=jax.ShapeDtypeStruct((BATCH, n_rows, D_MODEL), _F32),
        compiler_params=_params(2),
        name="post",
    )(*streams, oa, ob, mod_l, wo_bf, norm_w, wg_bf, wu_bf, wd_bf)


def _rope_tables():
    t = np.arange(SEQ)
    half = HEAD_DIM // 2
    inv_freq = np.float32(ROPE_THETA) ** (-np.arange(0, half, 2, dtype=np.float32) / np.float32(half))
    row = (t // GRID_W).astype(np.float32)[:, None] * inv_freq
    col = (t % GRID_W).astype(np.float32)[:, None] * inv_freq
    zeros = np.zeros((SEQ, HEAD_DIM // 4), np.float32)
    cos_head = np.concatenate([np.cos(row), np.cos(row), np.cos(col), np.cos(col)], axis=-1)
    sa_head = np.concatenate([-np.sin(row), zeros, -np.sin(col), zeros], axis=-1)
    sb_head = np.concatenate([zeros, np.sin(row), zeros, np.sin(col)], axis=-1)
    pad = lambda a, fill: jnp.asarray(np.concatenate(
        [np.tile(a, (1, LANES // HEAD_DIM)), np.full((CTX_LEN, LANES), fill, np.float32)],
        axis=0).astype(np.float32))
    return pad(cos_head, 1.0), pad(sa_head, 0.0), pad(sb_head, 0.0)


def _head_gains(qa, ka, qb, kb):
    scale = LOG2_E * HEAD_DIM ** -0.5
    ones = jnp.ones((W_NA,), _F32)
    return jnp.concatenate([
        jnp.tile(qa * scale, N_HEADS_NA), jnp.tile(ka, N_HEADS_NA), ones,
        jnp.tile(qb * scale, N_HEADS_GQA), jnp.tile(kb, N_KV_GQA), jnp.ones((W_KV,), _F32),
    ])[None, :]


def _softmax_shift(q_gain, k_gain, bias_max):
    q_norm_bound = LOG2_E * jnp.max(jnp.abs(q_gain)) * ROUNDING_SLACK
    key_norm_bound = HEAD_DIM ** 0.5 * jnp.max(jnp.abs(k_gain)) * ROUNDING_SLACK
    worst_gap = 2.0 * q_norm_bound * key_norm_bound + bias_max
    return jnp.stack([jnp.where(worst_gap <= SHIFT_RANGE, key_norm_bound, 0.0),
                      jnp.asarray(bias_max, _F32)]).astype(_F32)


def kernel(x, c, ctx, c_ctx, w_ada, b_ada, attn_norm, w_in, q_norm_a, k_norm_a, q_norm_b,
           k_norm_b, rpb, w_out, ffn_norm, w_gate, w_up, w_down):
    assert x.shape == (BATCH, SEQ, D_MODEL) and ctx.shape == (BATCH, CTX_LEN, D_MODEL)
    cond = jnp.concatenate(
        [c, c_ctx[None, :], jnp.zeros((COND_ROWS - BATCH - 1, D_MODEL), _F32)], axis=0)
    mod = _adaln(cond, w_ada, b_ada).reshape(DEPTH, COND_ROWS, 6, D_MODEL)
    bias = _bias_tables(rpb)
    cos_t, sa_t, sb_t = _rope_tables()
    lane = np.arange(LANES)
    ones_bd = jnp.asarray(lane[:, None] // HEAD_DIM == lane[None, :] // HEAD_DIM, _BF16)
    w_in_bf, w_out_bf = w_in.astype(_BF16), w_out.astype(_BF16)
    w_gate_bf, w_up_bf, w_down_bf = w_gate.astype(_BF16), w_up.astype(_BF16), w_down.astype(_BF16)

    streams = (x, ctx)
    for l in range(DEPTH):
        last = l == DEPTH - 1
        n_tiles = N_LAT_TILES if last else T_ALL // TILE
        gains = _head_gains(q_norm_a[l], k_norm_a[l], q_norm_b[l], k_norm_b[l])
        p = _inproj(streams, mod[l], attn_norm[l][None, :], w_in_bf, l, gains, ones_bd, cos_t, sa_t, sb_t)
        bias_max = LOG2_E * jnp.maximum(jnp.max(rpb[l]), 0.0)
        oa = _na_attention(_softmax_shift(q_norm_a[l], k_norm_a[l], bias_max), p, bias, l, n_tiles)
        ob = _gqa_attention(_softmax_shift(q_norm_b[l], k_norm_b[l], 0.0), p, n_tiles)
        xc = _post(streams, oa, ob, mod[l], w_out_bf, ffn_norm.reshape(DEPTH, 1, D_MODEL), w_gate_bf,
                   w_up_bf, w_down_bf, l, n_tiles)
        streams = (xc,)
    return xc
```

```python
import functools

import jax
import jax.numpy as jnp
import numpy as np
from jax import lax
from jax.experimental import pallas as pl
from jax.experimental.pallas import tpu as pltpu

D_MODEL = 1024
BATCH = 8
SEQ = 2048
DEPTH = 4
GRID_W = 64
CTX_LEN = 256
HEAD_DIM = 64
N_HEADS_NA = 8
N_HEADS_GQA = 8
N_KV_GQA = 2
GQA_GROUP = N_HEADS_GQA // N_KV_GQA
W_NA = N_HEADS_NA * HEAD_DIM
W_GQA = N_HEADS_GQA * HEAD_DIM
W_KV = N_KV_GQA * HEAD_DIM
IN_COLS = 3 * W_NA + W_GQA + 2 * W_KV
MAX_WIN_H = 8
WIN_W = 16
ROPE_THETA = 10000.0
D_FF = 2816
EPS = 1e-6

LANES = 128
T_ALL = SEQ + CTX_LEN
TILE = 256
STEP_ROWS = 512
N_LAT_STEPS = SEQ // STEP_ROWS
N_LAT_TILES = SEQ // TILE
ROWS_PER_TILE = TILE // GRID_W
GRID_ROWS = SEQ // GRID_W
WIN_ROWS = 12
WIN_KEYS = WIN_ROWS * GRID_W
N_DR = 2 * MAX_WIN_H - 1
N_DC = 2 * WIN_W - 1
MASK_VALUE = -1e30
LOG2_E = 1.4426950408889634
SHIFT_RANGE = 100.0
ROUNDING_SLACK = 1.01
COND_ROWS = 16

QA_BLK = 0
KA_BLK = W_NA // LANES
VA_BLK = 2 * W_NA // LANES
QB_BLK = 3 * W_NA // LANES
KB_BLK = (3 * W_NA + W_GQA) // LANES
VB_BLK = KB_BLK + 1
N_BLK = IN_COLS // LANES
PROJ_COLS = IN_COLS + 2 * W_KV
NORMED_BLKS = tuple(range(QA_BLK, VA_BLK)) + tuple(range(QB_BLK, VB_BLK))
ROPE_BLKS = tuple(range(QB_BLK, VB_BLK))

VMEM_LIMIT = 56 * 1024 * 1024

_BF16 = jnp.bfloat16
_F32 = jnp.float32


def _params(n_grid_dims):
    return pltpu.CompilerParams(
        dimension_semantics=("arbitrary",) * n_grid_dims,
        vmem_limit_bytes=VMEM_LIMIT)


def _split_bf16(a):
    hi = a.astype(_BF16)
    lo = (a - hi.astype(_F32)).astype(_BF16)
    return hi, lo


def _dot(a, b):
    return jnp.dot(a, b, preferred_element_type=_F32)


def _dot_nt(a, b):
    return lax.dot_general(a, b, (((1,), (1,)), ((), ())), preferred_element_type=_F32)


ADA_TN = 1024


def _adaln_kernel(cond_ref, w_ref, b_ref, o_ref):
    cond = cond_ref[...]
    a = cond / (1.0 + jnp.exp(-cond))
    a_hi, a_lo = _split_bf16(a)
    w_hi, w_lo = _split_bf16(w_ref[0])
    both = _dot(jnp.concatenate([a_hi, a_lo], axis=0), w_hi)
    acc = both[:COND_ROWS] + (both[COND_ROWS:] + _dot(a_hi, w_lo))
    o_ref[0] = acc + b_ref[0]


def _adaln(cond, w_ada, b_ada):
    n_out = 6 * D_MODEL
    return pl.pallas_call(
        _adaln_kernel,
        grid=(DEPTH, n_out // ADA_TN),
        in_specs=[
            pl.BlockSpec((COND_ROWS, D_MODEL), lambda l, n: (0, 0)),
            pl.BlockSpec((1, D_MODEL, ADA_TN), lambda l, n: (l, 0, n)),
            pl.BlockSpec((1, 1, ADA_TN), lambda l, n: (l, 0, n)),
        ],
        out_specs=pl.BlockSpec((1, COND_ROWS, ADA_TN), lambda l, n: (l, 0, n)),
        out_shape=jax.ShapeDtypeStruct((DEPTH, COND_ROWS, n_out), _F32),
        compiler_params=_params(2),
        name="adaln",
    )(cond, w_ada, b_ada.reshape(DEPTH, 1, n_out))


def _window_dr(case, i, j):
    if case == 0:
        return j - i + (MAX_WIN_H - 1) if j < MAX_WIN_H else None
    if case == 1:
        return j - i + 3 if i <= j < i + MAX_WIN_H else None
    return j - i - 1 if j >= WIN_ROWS - MAX_WIN_H else None


def _bias_kernel(rpb_ref, o_ref):
    l = pl.program_id(0)
    h = pl.program_id(1)
    cq = lax.broadcasted_iota(jnp.int32, (GRID_W, LANES), 0)
    ck = lax.broadcasted_iota(jnp.int32, (GRID_W, LANES), 1) % GRID_W
    cs = jnp.clip(cq - WIN_W // 2, 0, GRID_W - WIN_W)
    col_ok = (ck >= cs) & (ck < cs + WIN_W)
    dc = ck - cq + (WIN_W - 1)
    base = (l * N_HEADS_NA + h) * (N_DR * N_DC)
    toeplitz = []
    for dr in range(N_DR):
        t = jnp.full((GRID_W, LANES), MASK_VALUE, _F32)
        for d in range(N_DC):
            t = jnp.where(col_ok & (dc == d), rpb_ref[base + dr * N_DC + d] * LOG2_E, t)
        toeplitz.append(t)
    masked = jnp.full((GRID_W, LANES), MASK_VALUE, _F32)
    first_half = lax.broadcasted_iota(jnp.int32, (GRID_W, LANES), 1) < GRID_W
    for case in range(3):
        for i in range(ROWS_PER_TILE):
            for jj in range(WIN_ROWS // 2):
                dr0 = _window_dr(case, i, 2 * jj)
                dr1 = _window_dr(case, i, 2 * jj + 1)
                t0 = masked if dr0 is None else toeplitz[dr0]
                t1 = masked if dr1 is None else toeplitz[dr1]
                o_ref[0, case, 0, i * GRID_W:(i + 1) * GRID_W, jj * LANES:(jj + 1) * LANES] = (
                    jnp.where(first_half, t0, t1))


def _bias_tables(rpb):
    return pl.pallas_call(
        _bias_kernel,
        grid=(DEPTH, N_HEADS_NA),
        in_specs=[pl.BlockSpec(memory_space=pltpu.SMEM)],
        out_specs=pl.BlockSpec((1, 3, 1, TILE, WIN_KEYS), lambda l, h: (l, 0, h, 0, 0)),
        out_shape=jax.ShapeDtypeStruct((DEPTH, 3, N_HEADS_NA, TILE, WIN_KEYS), _F32),
        compiler_params=_params(2),
        name="bias_tables",
    )(rpb.reshape(-1))


def _stream_specs(n_streams):
    if n_streams == 1:
        return [pl.BlockSpec((1, STEP_ROWS, D_MODEL), lambda b, t: (b, t, 0))]
    return [pl.BlockSpec((1, STEP_ROWS, D_MODEL), lambda b, t: (b, jnp.minimum(t, N_LAT_STEPS - 1), 0)),
            pl.BlockSpec((1, CTX_LEN, D_MODEL), lambda b, t: (b, 0, 0))]


def _for_stream_rows(stream_refs, body):
    lat_ref, ctx_ref = stream_refs[0], stream_refs[-1]
    is_latent = pl.program_id(1) < N_LAT_STEPS

    @pl.when(is_latent)
    def _():
        body(lat_ref[0], slice(None))

    @pl.when(jnp.logical_not(is_latent))
    def _():
        body(ctx_ref[0, :CTX_LEN, :], slice(0, CTX_LEN))


def _inproj_kernel(n_streams, *refs):
    (mod_ref, nw_ref, w_f32_ref, gain_ref, ones_ref, cos_ref, sa_ref, sb_ref, o_ref,
     w_ref) = refs[n_streams:]

    @pl.when((pl.program_id(0) == 0) & (pl.program_id(1) == 0))
    def _():
        w_ref[...] = w_f32_ref[...].astype(_BF16)

    def project(x, rows):
        ms = jnp.mean(x * x, axis=-1, keepdims=True)
        h = x * lax.rsqrt(ms + EPS) * nw_ref[...]
        h = h * (1.0 + mod_ref[0, 1:2, :]) + mod_ref[0, 0:1, :]
        p = _dot(h.astype(_BF16), w_ref[...])
        ones_bd = ones_ref[...]
        for blk in range(N_BLK):
            sl = slice(blk * LANES, (blk + 1) * LANES)
            y = p[:, sl]
            if blk in NORMED_BLKS:
                sq_hi, sq_lo = _split_bf16(y * y)
                ss = _dot(sq_hi, ones_bd) + _dot(sq_lo, ones_bd)
                y = y * lax.rsqrt(ss * (1.0 / HEAD_DIM) + EPS) * gain_ref[:, sl]
            if blk in ROPE_BLKS:
                y = (y * cos_ref[rows, :]
                     + pltpu.roll(y, LANES - HEAD_DIM // 4, axis=1) * sa_ref[rows, :]
                     + pltpu.roll(y, HEAD_DIM // 4, axis=1) * sb_ref[rows, :])
            if blk < KB_BLK:
                o_ref[0, rows, sl] = y.astype(_BF16)
            else:
                swapped = pltpu.roll(y, HEAD_DIM, axis=1)
                first_half, _ = _half_masks()
                out_blk = KB_BLK + N_KV_GQA * (blk - KB_BLK)
                for head, dup in enumerate((jnp.where(first_half, y, swapped),
                                            jnp.where(first_half, swapped, y))):
                    o_ref[0, rows, (out_blk + head) * LANES:(out_blk + head + 1) * LANES] = (
                        dup.astype(_BF16))

    _for_stream_rows(refs[:n_streams], project)


def _inproj(streams, mod_l, norm_w, w_in, layer, gains, ones_bd, cos_t, sa_t, sb_t):
    mod_row = lambda b, t: (jnp.where(t >= N_LAT_STEPS, BATCH, b), 0, 0)
    table = pl.BlockSpec((STEP_ROWS, LANES), lambda b, t: (t, 0))
    return pl.pallas_call(
        functools.partial(_inproj_kernel, len(streams)),
        grid=(BATCH, pl.cdiv(T_ALL, STEP_ROWS)),
        in_specs=[
            *_stream_specs(len(streams)),
            pl.BlockSpec((1, 6, D_MODEL), mod_row),
            pl.BlockSpec((1, D_MODEL), lambda b, t: (0, 0)),
            pl.BlockSpec((None, D_MODEL, IN_COLS), lambda b, t: (layer, 0, 0),
                         pipeline_mode=pl.Buffered(1)),
            pl.BlockSpec((1, IN_COLS), lambda b, t: (0, 0)),
            pl.BlockSpec((LANES, LANES), lambda b, t: (0, 0)),
            table, table, table,
        ],
        out_specs=pl.BlockSpec((1, STEP_ROWS, PROJ_COLS), lambda b, t: (b, t, 0)),
        out_shape=jax.ShapeDtypeStruct((BATCH, T_ALL, PROJ_COLS), _BF16),
        scratch_shapes=[pltpu.VMEM((D_MODEL, IN_COLS), _BF16)],
        compiler_params=_params(2),
        name="inproj",
    )(*streams, mod_l, norm_w, w_in, gains, ones_bd, cos_t, sa_t, sb_t)


def _half_masks():
    lane = lax.broadcasted_iota(jnp.int32, (1, LANES), 1)
    return lane < HEAD_DIM, lane >= HEAD_DIM


def _pair_scores(q, parts, shift):
    n_q = q.shape[0]
    q2 = jnp.concatenate([jnp.where(keep, q, jnp.zeros_like(q)) for keep in _half_masks()], axis=0)
    scores = []
    for k, _, bias in parts:
        s = _dot_nt(q2, k)
        if bias is not None:
            s = s + bias[...].reshape(2 * n_q, -1)
        scores.append(s)
    if shift is None:
        m = functools.reduce(jnp.maximum, [jnp.max(s, axis=-1, keepdims=True) for s in scores])
    else:
        key_norm_bound, bias_max = shift
        q2f = q2.astype(_F32)
        m = jnp.sqrt(jnp.sum(q2f * q2f, axis=-1, keepdims=True)) * key_norm_bound + bias_max
    return scores, m


def _pair_finish(scores, m, parts):
    n_q = m.shape[0] // 2
    denom = 0.0
    acc = 0.0
    for s, (_, v, _) in zip(scores, parts):
        e = jnp.exp2(s - m)
        denom = denom + jnp.sum(e, axis=-1, keepdims=True)
        acc = acc + _dot(e.astype(_BF16), v)
    out = acc / denom
    first_half, _ = _half_masks()
    return jnp.where(first_half, out[:n_q], out[n_q:])


def _attend_jobs(jobs, shift, o_ref, lookahead):
    pending = []

    def finish(dest, state):
        o_ref[(0,) + dest] = _pair_finish(*state).astype(o_ref.dtype)

    for dest, thunk in jobs:
        q, parts = thunk()
        pending.append((dest, _pair_scores(q, parts, shift) + (parts,)))
        if len(pending) > lookahead:
            finish(*pending.pop(0))
    for item in pending:
        finish(*item)


def _lane_blocks(width):
    return [(blk, slice(blk * LANES, (blk + 1) * LANES)) for blk in range(width // LANES)]


def _na_kernel(shift_ref, q_ref, k_ref, v_ref, kc_ref, vc_ref, bias_ref, o_ref):
    step = pl.program_id(1)
    is_latent = step < N_LAT_STEPS
    bounded = shift_ref[0] > 0.0

    def ctx_part(sl):
        return (kc_ref[0, :, sl], vc_ref[0, :, sl], None)

    def latent(shift):
        jobs = []
        for sub in range(STEP_ROWS // TILE):
            g = step * (STEP_ROWS // TILE) + sub
            first_row = jnp.clip(g * ROWS_PER_TILE - MAX_WIN_H // 2, 0, GRID_ROWS - WIN_ROWS)
            rows = pl.ds(pl.multiple_of(first_row * GRID_W, GRID_W), WIN_KEYS)
            case = jnp.where(g == 0, 0, jnp.where(g == N_LAT_TILES - 1, 2, 1))
            q_rows = slice(sub * TILE, (sub + 1) * TILE)
            for hp, sl in _lane_blocks(W_NA):
                def thunk(hp=hp, sl=sl, rows=rows, case=case, q_rows=q_rows):
                    win_part = (k_ref[0, rows, sl], v_ref[0, rows, sl],
                                bias_ref.at[case, pl.ds(2 * hp, 2)])
                    return q_ref[0, q_rows, sl], [win_part, ctx_part(sl)]
                jobs.append(((q_rows, sl), thunk))
        _attend_jobs(jobs, shift, o_ref, lookahead=1)

    @pl.when(is_latent & bounded)
    def _():
        latent((shift_ref[0], shift_ref[1]))

    @pl.when(is_latent & jnp.logical_not(bounded))
    def _():
        latent(None)

    @pl.when(jnp.logical_not(is_latent))
    def _():
        q_rows = slice(0, CTX_LEN)
        jobs = [((q_rows, sl), lambda sl=sl: (q_ref[0, q_rows, sl], [ctx_part(sl)]))
                for _, sl in _lane_blocks(W_NA)]
        _attend_jobs(jobs, None, o_ref, lookahead=1)


def _na_attention(shift, p, bias, layer, n_tiles):
    ctx_tile = N_LAT_TILES
    return pl.pallas_call(
        _na_kernel,
        grid=(BATCH, pl.cdiv(n_tiles * TILE, STEP_ROWS)),
        in_specs=[
            pl.BlockSpec(memory_space=pltpu.SMEM),
            pl.BlockSpec((1, STEP_ROWS, W_NA), lambda b, g: (b, g, QA_BLK * LANES // W_NA)),
            pl.BlockSpec((1, SEQ, W_NA), lambda b, g: (b, 0, KA_BLK * LANES // W_NA)),
            pl.BlockSpec((1, SEQ, W_NA), lambda b, g: (b, 0, VA_BLK * LANES // W_NA)),
            pl.BlockSpec((1, CTX_LEN, W_NA), lambda b, g: (b, ctx_tile, KA_BLK * LANES // W_NA)),
            pl.BlockSpec((1, CTX_LEN, W_NA), lambda b, g: (b, ctx_tile, VA_BLK * LANES // W_NA)),
            pl.BlockSpec((None, 3, N_HEADS_NA, TILE, WIN_KEYS), lambda b, g: (layer, 0, 0, 0, 0),
                         pipeline_mode=pl.Buffered(1)),
        ],
        out_specs=pl.BlockSpec((1, STEP_ROWS, W_NA), lambda b, g: (b, g, 0)),
        out_shape=jax.ShapeDtypeStruct((BATCH, n_tiles * TILE, W_NA), _BF16),
        compiler_params=_params(2),
        name="na_attention",
    )(shift, p, p, p, p, p, bias)


def _gqa_kernel(shift_ref, q_ref, *refs):
    k_refs, v_refs, o_ref = refs[:N_KV_GQA], refs[N_KV_GQA:2 * N_KV_GQA], refs[-1]
    is_latent = pl.program_id(1) < N_LAT_STEPS
    bounded = shift_ref[0] > 0.0
    blocks_per_kv = W_GQA // LANES // N_KV_GQA

    def attend(q_rows, kv_rows, shift):
        def thunk(j, sl):
            kv = j // blocks_per_kv
            return q_ref[0, q_rows, sl], [(k_refs[kv][0, kv_rows, :], v_refs[kv][0, kv_rows, :], None)]
        jobs = [((q_rows, sl), functools.partial(thunk, j, sl)) for j, sl in _lane_blocks(W_GQA)]
        _attend_jobs(jobs, shift, o_ref, lookahead=0)

    @pl.when(is_latent & bounded)
    def _():
        attend(slice(None), slice(None), (shift_ref[0], shift_ref[1]))

    @pl.when(is_latent & jnp.logical_not(bounded))
    def _():
        attend(slice(None), slice(None), None)

    @pl.when(jnp.logical_not(is_latent))
    def _():
        attend(slice(0, CTX_LEN), slice(SEQ, T_ALL), None)


def _gqa_attention(shift, p, n_tiles):
    kv_spec = lambda blk: pl.BlockSpec((1, T_ALL, LANES), lambda b, t: (b, 0, blk))
    return pl.pallas_call(
        _gqa_kernel,
        grid=(BATCH, pl.cdiv(n_tiles * TILE, STEP_ROWS)),
        in_specs=[
            pl.BlockSpec(memory_space=pltpu.SMEM),
            pl.BlockSpec((1, STEP_ROWS, W_GQA), lambda b, t: (b, t, QB_BLK * LANES // W_GQA)),
            *[kv_spec(KB_BLK + head) for head in range(N_KV_GQA)],
            *[kv_spec(KB_BLK + N_KV_GQA + head) for head in range(N_KV_GQA)],
        ],
        out_specs=pl.BlockSpec((1, STEP_ROWS, W_GQA), lambda b, t: (b, t, 0)),
        out_shape=jax.ShapeDtypeStruct((BATCH, n_tiles * TILE, W_GQA), _BF16),
        compiler_params=_params(2),
        name="gqa_attention",
    )(shift, p, *([p] * (2 * N_KV_GQA)))


def _post_kernel(n_streams, *refs):
    oa_ref, ob_ref, mod_ref, wo_ref, nw_ref, wg_ref, wu_ref, wd_ref, out_ref = refs[n_streams:]

    def update(x, rows):
        attn = _dot(oa_ref[0, rows, :], wo_ref[:W_NA, :]) + _dot(ob_ref[0, rows, :], wo_ref[W_NA:, :])
        x = x + mod_ref[0, 2:3, :] * attn
        ms = jnp.mean(x * x, axis=-1, keepdims=True)
        h = x * lax.rsqrt(ms + EPS) * nw_ref[...]
        h = (h * (1.0 + mod_ref[0, 4:5, :]) + mod_ref[0, 3:4, :]).astype(_BF16)
        gate = _dot(h, wg_ref[...])
        up = _dot(h, wu_ref[...])
        act = (gate / (1.0 + jnp.exp(-gate)) * up).astype(_BF16)
        out_ref[0, rows, :] = x + mod_ref[0, 5:6, :] * _dot(act, wd_ref[...])

    _for_stream_rows(refs[:n_streams], update)


def _post(streams, oa, ob, mod_l, wo_bf, norm_w, wg_bf, wu_bf, wd_bf, layer, n_tiles):
    n_rows = n_tiles * TILE
    mod_row = lambda b, t: (jnp.where(t >= N_LAT_STEPS, BATCH, b), 0, 0)
    resident = lambda shape: pl.BlockSpec((None,) + shape, lambda b, t: (layer,) + (0,) * len(shape),
                                          pipeline_mode=pl.Buffered(1))
    rows_of = lambda width: pl.BlockSpec((1, STEP_ROWS, width), lambda b, t: (b, t, 0))
    return pl.pallas_call(
        functools.partial(_post_kernel, len(streams)),
        grid=(BATCH, pl.cdiv(n_rows, STEP_ROWS)),
        in_specs=[
            *_stream_specs(len(streams)),
            rows_of(W_NA),
            rows_of(W_GQA),
            pl.BlockSpec((1, 6, D_MODEL), mod_row),
            resident((W_NA + W_GQA, D_MODEL)),
            resident((1, D_MODEL)),
            resident((D_MODEL, D_FF)),
            resident((D_MODEL, D_FF)),
            resident((D_FF, D_MODEL)),
        ],
        out_specs=rows_of(D_MODEL),
        out_shape=jax.ShapeDtypeStruct((BATCH, n_rows, D_MODEL), _F32),
        compiler_params=_params(2),
        name="post",
    )(*streams, oa, ob, mod_l, wo_bf, norm_w, wg_bf, wu_bf, wd_bf)


def _rope_tables():
    t = np.arange(SEQ)
    half = HEAD_DIM // 2
    inv_freq = np.float32(ROPE_THETA) ** (-np.arange(0, half, 2, dtype=np.float32) / np.float32(half))
    row = (t // GRID_W).astype(np.float32)[:, None] * inv_freq
    col = (t % GRID_W).astype(np.float32)[:, None] * inv_freq
    zeros = np.zeros((SEQ, HEAD_DIM // 4), np.float32)
    cos_head = np.concatenate([np.cos(row), np.cos(row), np.cos(col), np.cos(col)], axis=-1)
    sa_head = np.concatenate([-np.sin(row), zeros, -np.sin(col), zeros], axis=-1)
    sb_head = np.concatenate([zeros, np.sin(row), zeros, np.sin(col)], axis=-1)
    pad = lambda a, fill: jnp.asarray(np.concatenate(
        [np.tile(a, (1, LANES // HEAD_DIM)), np.full((CTX_LEN, LANES), fill, np.float32)],
        axis=0).astype(np.float32))
    return pad(cos_head, 1.0), pad(sa_head, 0.0), pad(sb_head, 0.0)


def _head_gains(qa, ka, qb, kb):
    scale = LOG2_E * HEAD_DIM ** -0.5
    ones = jnp.ones((W_NA,), _F32)
    return jnp.concatenate([
        jnp.tile(qa * scale, N_HEADS_NA), jnp.tile(ka, N_HEADS_NA), ones,
        jnp.tile(qb * scale, N_HEADS_GQA), jnp.tile(kb, N_KV_GQA), jnp.ones((W_KV,), _F32),
    ])[None, :]


def _softmax_shift(q_gain, k_gain, bias_max):
    q_norm_bound = LOG2_E * jnp.max(jnp.abs(q_gain)) * ROUNDING_SLACK
    key_norm_bound = HEAD_DIM ** 0.5 * jnp.max(jnp.abs(k_gain)) * ROUNDING_SLACK
    worst_gap = 2.0 * q_norm_bound * key_norm_bound + bias_max
    return jnp.stack([jnp.where(worst_gap <= SHIFT_RANGE, key_norm_bound, 0.0),
                      jnp.asarray(bias_max, _F32)]).astype(_F32)


def kernel(x, c, ctx, c_ctx, w_ada, b_ada, attn_norm, w_in, q_norm_a, k_norm_a, q_norm_b,
           k_norm_b, rpb, w_out, ffn_norm, w_gate, w_up, w_down):
    assert x.shape == (BATCH, SEQ, D_MODEL) and ctx.shape == (BATCH, CTX_LEN, D_MODEL)
    cond = jnp.concatenate(
        [c, c_ctx[None, :], jnp.zeros((COND_ROWS - BATCH - 1, D_MODEL), _F32)], axis=0)
    mod = _adaln(cond, w_ada, b_ada).reshape(DEPTH, COND_ROWS, 6, D_MODEL)
    bias = _bias_tables(rpb)
    cos_t, sa_t, sb_t = _rope_tables()
    lane = np.arange(LANES)
    ones_bd = jnp.asarray(lane[:, None] // HEAD_DIM == lane[None, :] // HEAD_DIM, _BF16)
    w_out_bf = w_out.astype(_BF16)
    w_gate_bf, w_up_bf, w_down_bf = w_gate.astype(_BF16), w_up.astype(_BF16), w_down.astype(_BF16)

    streams = (x, ctx)
    for l in range(DEPTH):
        last = l == DEPTH - 1
        n_tiles = N_LAT_TILES if last else T_ALL // TILE
        gains = _head_gains(q_norm_a[l], k_norm_a[l], q_norm_b[l], k_norm_b[l])
        p = _inproj(streams, mod[l], attn_norm[l][None, :], w_in, l, gains, ones_bd, cos_t, sa_t, sb_t)
        bias_max = LOG2_E * jnp.maximum(jnp.max(rpb[l]), 0.0)
        oa = _na_attention(_softmax_shift(q_norm_a[l], k_norm_a[l], bias_max), p, bias, l, n_tiles)
        ob = _gqa_attention(_softmax_shift(q_norm_b[l], k_norm_b[l], 0.0), p, n_tiles)
        xc = _post(streams, oa, ob, mod[l], w_out_bf, ffn_norm.reshape(DEPTH, 1, D_MODEL), w_gate_bf,
                   w_up_bf, w_down_bf, l, n_tiles)
        streams = (xc,)
    return xc
```

```python
import functools

import jax
import jax.numpy as jnp
import numpy as np
from jax import lax
from jax.experimental import pallas as pl
from jax.experimental.pallas import tpu as pltpu

D_MODEL = 1024
BATCH = 8
SEQ = 2048
DEPTH = 4
GRID_W = 64
CTX_LEN = 256
HEAD_DIM = 64
N_HEADS_NA = 8
N_HEADS_GQA = 8
N_KV_GQA = 2
GQA_GROUP = N_HEADS_GQA // N_KV_GQA
W_NA = N_HEADS_NA * HEAD_DIM
W_GQA = N_HEADS_GQA * HEAD_DIM
W_KV = N_KV_GQA * HEAD_DIM
IN_COLS = 3 * W_NA + W_GQA + 2 * W_KV
MAX_WIN_H = 8
WIN_W = 16
ROPE_THETA = 10000.0
D_FF = 2816
EPS = 1e-6

LANES = 128
T_ALL = SEQ + CTX_LEN
TILE = 256
STEP_ROWS = 512
N_LAT_STEPS = SEQ // STEP_ROWS
N_LAT_TILES = SEQ // TILE
ROWS_PER_TILE = TILE // GRID_W
GRID_ROWS = SEQ // GRID_W
WIN_ROWS = 12
WIN_KEYS = WIN_ROWS * GRID_W
N_DR = 2 * MAX_WIN_H - 1
N_DC = 2 * WIN_W - 1
BIAS_HEADS_PER_STEP = 4
MASK_VALUE = -1e30
LOG2_E = 1.4426950408889634
SHIFT_RANGE = 100.0
ROUNDING_SLACK = 1.01
COND_ROWS = 16

QA_BLK = 0
KA_BLK = W_NA // LANES
VA_BLK = 2 * W_NA // LANES
QB_BLK = 3 * W_NA // LANES
KB_BLK = (3 * W_NA + W_GQA) // LANES
VB_BLK = KB_BLK + 1
N_BLK = IN_COLS // LANES
PROJ_COLS = IN_COLS + 2 * W_KV
NORMED_BLKS = tuple(range(QA_BLK, VA_BLK)) + tuple(range(QB_BLK, VB_BLK))
ROPE_BLKS = tuple(range(QB_BLK, VB_BLK))

VMEM_LIMIT = 56 * 1024 * 1024

_BF16 = jnp.bfloat16
_F32 = jnp.float32


def _params(n_grid_dims):
    return pltpu.CompilerParams(
        dimension_semantics=("arbitrary",) * n_grid_dims,
        vmem_limit_bytes=VMEM_LIMIT)


def _split_bf16(a):
    hi = a.astype(_BF16)
    lo = (a - hi.astype(_F32)).astype(_BF16)
    return hi, lo


def _dot(a, b):
    return jnp.dot(a, b, preferred_element_type=_F32)


def _dot_nt(a, b):
    return lax.dot_general(a, b, (((1,), (1,)), ((), ())), preferred_element_type=_F32)


ADA_TN = 1024


def _adaln_kernel(cond_ref, w_ref, b_ref, o_ref):
    cond = cond_ref[...]
    a = cond / (1.0 + jnp.exp(-cond))
    a_hi, a_lo = _split_bf16(a)
    w_hi, w_lo = _split_bf16(w_ref[0])
    both = _dot(jnp.concatenate([a_hi, a_lo], axis=0), w_hi)
    acc = both[:COND_ROWS] + (both[COND_ROWS:] + _dot(a_hi, w_lo))
    o_ref[0] = acc + b_ref[0]


def _adaln(cond, w_ada, b_ada):
    n_out = 6 * D_MODEL
    return pl.pallas_call(
        _adaln_kernel,
        grid=(DEPTH, n_out // ADA_TN),
        in_specs=[
            pl.BlockSpec((COND_ROWS, D_MODEL), lambda l, n: (0, 0)),
            pl.BlockSpec((1, D_MODEL, ADA_TN), lambda l, n: (l, 0, n)),
            pl.BlockSpec((1, 1, ADA_TN), lambda l, n: (l, 0, n)),
        ],
        out_specs=pl.BlockSpec((1, COND_ROWS, ADA_TN), lambda l, n: (l, 0, n)),
        out_shape=jax.ShapeDtypeStruct((DEPTH, COND_ROWS, n_out), _F32),
        compiler_params=_params(2),
        name="adaln",
    )(cond, w_ada, b_ada.reshape(DEPTH, 1, n_out))


def _window_dr(case, i, j):
    if case == 0:
        return j - i + (MAX_WIN_H - 1) if j < MAX_WIN_H else None
    if case == 1:
        return j - i + 3 if i <= j < i + MAX_WIN_H else None
    return j - i - 1 if j >= WIN_ROWS - MAX_WIN_H else None


def _bias_kernel(rpb_ref, o_ref):
    for head in range(BIAS_HEADS_PER_STEP):
        _bias_head(rpb_ref, o_ref, head)


def _bias_head(rpb_ref, o_ref, head):
    l = pl.program_id(0)
    h = pl.program_id(1) * BIAS_HEADS_PER_STEP + head
    cq = lax.broadcasted_iota(jnp.int32, (GRID_W, LANES), 0)
    ck = lax.broadcasted_iota(jnp.int32, (GRID_W, LANES), 1) % GRID_W
    cs = jnp.clip(cq - WIN_W // 2, 0, GRID_W - WIN_W)
    col_ok = (ck >= cs) & (ck < cs + WIN_W)
    dc = ck - cq + (WIN_W - 1)
    base = (l * N_HEADS_NA + h) * (N_DR * N_DC)
    toeplitz = []
    for dr in range(N_DR):
        t = jnp.full((GRID_W, LANES), MASK_VALUE, _F32)
        for d in range(N_DC):
            t = jnp.where(col_ok & (dc == d), rpb_ref[base + dr * N_DC + d] * LOG2_E, t)
        toeplitz.append(t)
    masked = jnp.full((GRID_W, LANES), MASK_VALUE, _F32)
    first_half = lax.broadcasted_iota(jnp.int32, (GRID_W, LANES), 1) < GRID_W
    for case in range(3):
        for i in range(ROWS_PER_TILE):
            for jj in range(WIN_ROWS // 2):
                dr0 = _window_dr(case, i, 2 * jj)
                dr1 = _window_dr(case, i, 2 * jj + 1)
                t0 = masked if dr0 is None else toeplitz[dr0]
                t1 = masked if dr1 is None else toeplitz[dr1]
                o_ref[0, case, head, i * GRID_W:(i + 1) * GRID_W, jj * LANES:(jj + 1) * LANES] = (
                    jnp.where(first_half, t0, t1))


def _bias_tables(rpb):
    return pl.pallas_call(
        _bias_kernel,
        grid=(DEPTH, N_HEADS_NA // BIAS_HEADS_PER_STEP),
        in_specs=[pl.BlockSpec(memory_space=pltpu.SMEM)],
        out_specs=pl.BlockSpec((1, 3, BIAS_HEADS_PER_STEP, TILE, WIN_KEYS), lambda l, h: (l, 0, h, 0, 0)),
        out_shape=jax.ShapeDtypeStruct((DEPTH, 3, N_HEADS_NA, TILE, WIN_KEYS), _F32),
        compiler_params=_params(2),
        name="bias_tables",
    )(rpb.reshape(-1))


def _stream_specs(n_streams):
    if n_streams == 1:
        return [pl.BlockSpec((1, STEP_ROWS, D_MODEL), lambda b, t: (b, t, 0))]
    return [pl.BlockSpec((1, STEP_ROWS, D_MODEL), lambda b, t: (b, jnp.minimum(t, N_LAT_STEPS - 1), 0)),
            pl.BlockSpec((1, CTX_LEN, D_MODEL), lambda b, t: (b, 0, 0))]


def _for_stream_rows(stream_refs, body):
    lat_ref, ctx_ref = stream_refs[0], stream_refs[-1]
    is_latent = pl.program_id(1) < N_LAT_STEPS

    @pl.when(is_latent)
    def _():
        body(lat_ref[0], slice(None))

    @pl.when(jnp.logical_not(is_latent))
    def _():
        body(ctx_ref[0, :CTX_LEN, :], slice(0, CTX_LEN))


def _inproj_kernel(n_streams, *refs):
    (mod_ref, nw_ref, w_f32_ref, gain_ref, ones_ref, cos_ref, sa_ref, sb_ref, o_ref,
     w_ref) = refs[n_streams:]

    @pl.when((pl.program_id(0) == 0) & (pl.program_id(1) == 0))
    def _():
        w_ref[...] = w_f32_ref[...].astype(_BF16)

    def project(x, rows):
        ms = jnp.mean(x * x, axis=-1, keepdims=True)
        h = x * lax.rsqrt(ms + EPS) * nw_ref[...]
        h = h * (1.0 + mod_ref[0, 1:2, :]) + mod_ref[0, 0:1, :]
        p = _dot(h.astype(_BF16), w_ref[...])
        ones_bd = ones_ref[...]
        for blk in range(N_BLK):
            sl = slice(blk * LANES, (blk + 1) * LANES)
            y = p[:, sl]
            if blk in NORMED_BLKS:
                sq_hi, sq_lo = _split_bf16(y * y)
                ss = _dot(sq_hi, ones_bd) + _dot(sq_lo, ones_bd)
                y = y * lax.rsqrt(ss * (1.0 / HEAD_DIM) + EPS) * gain_ref[:, sl]
            if blk in ROPE_BLKS:
                y = (y * cos_ref[rows, :]
                     + pltpu.roll(y, LANES - HEAD_DIM // 4, axis=1) * sa_ref[rows, :]
                     + pltpu.roll(y, HEAD_DIM // 4, axis=1) * sb_ref[rows, :])
            if blk < KB_BLK:
                o_ref[0, rows, sl] = y.astype(_BF16)
            else:
                swapped = pltpu.roll(y, HEAD_DIM, axis=1)
                first_half, _ = _half_masks()
                out_blk = KB_BLK + N_KV_GQA * (blk - KB_BLK)
                for head, dup in enumerate((jnp.where(first_half, y, swapped),
                                            jnp.where(first_half, swapped, y))):
                    o_ref[0, rows, (out_blk + head) * LANES:(out_blk + head + 1) * LANES] = (
                        dup.astype(_BF16))

    _for_stream_rows(refs[:n_streams], project)


def _inproj(streams, mod_l, norm_w, w_in, layer, gains, ones_bd, cos_t, sa_t, sb_t):
    mod_row = lambda b, t: (jnp.where(t >= N_LAT_STEPS, BATCH, b), 0, 0)
    table = pl.BlockSpec((STEP_ROWS, LANES), lambda b, t: (t, 0))
    return pl.pallas_call(
        functools.partial(_inproj_kernel, len(streams)),
        grid=(BATCH, pl.cdiv(T_ALL, STEP_ROWS)),
        in_specs=[
            *_stream_specs(len(streams)),
            pl.BlockSpec((1, 6, D_MODEL), mod_row),
            pl.BlockSpec((1, D_MODEL), lambda b, t: (0, 0)),
            pl.BlockSpec((None, D_MODEL, IN_COLS), lambda b, t: (layer, 0, 0),
                         pipeline_mode=pl.Buffered(1)),
            pl.BlockSpec((1, IN_COLS), lambda b, t: (0, 0)),
            pl.BlockSpec((LANES, LANES), lambda b, t: (0, 0)),
            table, table, table,
        ],
        out_specs=pl.BlockSpec((1, STEP_ROWS, PROJ_COLS), lambda b, t: (b, t, 0)),
        out_shape=jax.ShapeDtypeStruct((BATCH, T_ALL, PROJ_COLS), _BF16),
        scratch_shapes=[pltpu.VMEM((D_MODEL, IN_COLS), _BF16)],
        compiler_params=_params(2),
        name="inproj",
    )(*streams, mod_l, norm_w, w_in, gains, ones_bd, cos_t, sa_t, sb_t)


def _half_masks():
    lane = lax.broadcasted_iota(jnp.int32, (1, LANES), 1)
    return lane < HEAD_DIM, lane >= HEAD_DIM


def _pair_scores(q, parts, shift):
    n_q = q.shape[0]
    q2 = jnp.concatenate([jnp.where(keep, q, jnp.zeros_like(q)) for keep in _half_masks()], axis=0)
    scores = []
    for k, _, bias in parts:
        s = _dot_nt(q2, k)
        if bias is not None:
            s = s + bias[...].reshape(2 * n_q, -1)
        scores.append(s)
    if shift is None:
        m = functools.reduce(jnp.maximum, [jnp.max(s, axis=-1, keepdims=True) for s in scores])
    else:
        key_norm_bound, bias_max = shift
        q2f = q2.astype(_F32)
        m = jnp.sqrt(jnp.sum(q2f * q2f, axis=-1, keepdims=True)) * key_norm_bound + bias_max
    return scores, m


def _pair_finish(scores, m, parts):
    n_q = m.shape[0] // 2
    denom = 0.0
    acc = 0.0
    for s, (_, v, _) in zip(scores, parts):
        e = jnp.exp2(s - m)
        denom = denom + jnp.sum(e, axis=-1, keepdims=True)
        acc = acc + _dot(e.astype(_BF16), v)
    out = acc / denom
    first_half, _ = _half_masks()
    return jnp.where(first_half, out[:n_q], out[n_q:])


def _attend_jobs(jobs, shift, o_ref, lookahead):
    pending = []

    def finish(dest, state):
        o_ref[(0,) + dest] = _pair_finish(*state).astype(o_ref.dtype)

    for dest, thunk in jobs:
        q, parts = thunk()
        pending.append((dest, _pair_scores(q, parts, shift) + (parts,)))
        if len(pending) > lookahead:
            finish(*pending.pop(0))
    for item in pending:
        finish(*item)


def _lane_blocks(width):
    return [(blk, slice(blk * LANES, (blk + 1) * LANES)) for blk in range(width // LANES)]


def _na_kernel(shift_ref, q_ref, k_ref, v_ref, kc_ref, vc_ref, bias_ref, o_ref):
    step = pl.program_id(1)
    is_latent = step < N_LAT_STEPS
    bounded = shift_ref[0] > 0.0

    def ctx_part(sl):
        return (kc_ref[0, :, sl], vc_ref[0, :, sl], None)

    def latent(shift):
        jobs = []
        for sub in range(STEP_ROWS // TILE):
            g = step * (STEP_ROWS // TILE) + sub
            first_row = jnp.clip(g * ROWS_PER_TILE - MAX_WIN_H // 2, 0, GRID_ROWS - WIN_ROWS)
            rows = pl.ds(pl.multiple_of(first_row * GRID_W, GRID_W), WIN_KEYS)
            case = jnp.where(g == 0, 0, jnp.where(g == N_LAT_TILES - 1, 2, 1))
            q_rows = slice(sub * TILE, (sub + 1) * TILE)
            for hp, sl in _lane_blocks(W_NA):
                def thunk(hp=hp, sl=sl, rows=rows, case=case, q_rows=q_rows):
                    win_part = (k_ref[0, rows, sl], v_ref[0, rows, sl],
                                bias_ref.at[case, pl.ds(2 * hp, 2)])
                    return q_ref[0, q_rows, sl], [win_part, ctx_part(sl)]
                jobs.append(((q_rows, sl), thunk))
        _attend_jobs(jobs, shift, o_ref, lookahead=1)

    @pl.when(is_latent & bounded)
    def _():
        latent((shift_ref[0], shift_ref[1]))

    @pl.when(is_latent & jnp.logical_not(bounded))
    def _():
        latent(None)

    @pl.when(jnp.logical_not(is_latent))
    def _():
        q_rows = slice(0, CTX_LEN)
        jobs = [((q_rows, sl), lambda sl=sl: (q_ref[0, q_rows, sl], [ctx_part(sl)]))
                for _, sl in _lane_blocks(W_NA)]
        _attend_jobs(jobs, None, o_ref, lookahead=1)


def _na_attention(shift, p, bias, layer, n_tiles):
    ctx_tile = N_LAT_TILES
    return pl.pallas_call(
        _na_kernel,
        grid=(BATCH, pl.cdiv(n_tiles * TILE, STEP_ROWS)),
        in_specs=[
            pl.BlockSpec(memory_space=pltpu.SMEM),
            pl.BlockSpec((1, STEP_ROWS, W_NA), lambda b, g: (b, g, QA_BLK * LANES // W_NA)),
            pl.BlockSpec((1, SEQ, W_NA), lambda b, g: (b, 0, KA_BLK * LANES // W_NA)),
            pl.BlockSpec((1, SEQ, W_NA), lambda b, g: (b, 0, VA_BLK * LANES // W_NA)),
            pl.BlockSpec((1, CTX_LEN, W_NA), lambda b, g: (b, ctx_tile, KA_BLK * LANES // W_NA)),
            pl.BlockSpec((1, CTX_LEN, W_NA), lambda b, g: (b, ctx_tile, VA_BLK * LANES // W_NA)),
            pl.BlockSpec((None, 3, N_HEADS_NA, TILE, WIN_KEYS), lambda b, g: (layer, 0, 0, 0, 0),
                         pipeline_mode=pl.Buffered(1)),
        ],
        out_specs=pl.BlockSpec((1, STEP_ROWS, W_NA), lambda b, g: (b, g, 0)),
        out_shape=jax.ShapeDtypeStruct((BATCH, n_tiles * TILE, W_NA), _BF16),
        compiler_params=_params(2),
        name="na_attention",
    )(shift, p, p, p, p, p, bias)


def _gqa_kernel(shift_ref, q_ref, *refs):
    k_refs, v_refs, o_ref = refs[:N_KV_GQA], refs[N_KV_GQA:2 * N_KV_GQA], refs[-1]
    is_latent = pl.program_id(1) < N_LAT_STEPS
    bounded = shift_ref[0] > 0.0
    blocks_per_kv = W_GQA // LANES // N_KV_GQA

    def attend(q_rows, kv_rows, shift):
        def thunk(j, sl):
            kv = j // blocks_per_kv
            return q_ref[0, q_rows, sl], [(k_refs[kv][0, kv_rows, :], v_refs[kv][0, kv_rows, :], None)]
        jobs = [((q_rows, sl), functools.partial(thunk, j, sl)) for j, sl in _lane_blocks(W_GQA)]
        _attend_jobs(jobs, shift, o_ref, lookahead=0)

    @pl.when(is_latent & bounded)
    def _():
        attend(slice(None), slice(None), (shift_ref[0], shift_ref[1]))

    @pl.when(is_latent & jnp.logical_not(bounded))
    def _():
        attend(slice(None), slice(None), None)

    @pl.when(jnp.logical_not(is_latent))
    def _():
        attend(slice(0, CTX_LEN), slice(SEQ, T_ALL), None)


def _gqa_attention(shift, p, n_tiles):
    kv_spec = lambda blk: pl.BlockSpec((1, T_ALL, LANES), lambda b, t: (b, 0, blk))
    return pl.pallas_call(
        _gqa_kernel,
        grid=(BATCH, pl.cdiv(n_tiles * TILE, STEP_ROWS)),
        in_specs=[
            pl.BlockSpec(memory_space=pltpu.SMEM),
            pl.BlockSpec((1, STEP_ROWS, W_GQA), lambda b, t: (b, t, QB_BLK * LANES // W_GQA)),
            *[kv_spec(KB_BLK + head) for head in range(N_KV_GQA)],
            *[kv_spec(KB_BLK + N_KV_GQA + head) for head in range(N_KV_GQA)],
        ],
        out_specs=pl.BlockSpec((1, STEP_ROWS, W_GQA), lambda b, t: (b, t, 0)),
        out_shape=jax.ShapeDtypeStruct((BATCH, n_tiles * TILE, W_GQA), _BF16),
        compiler_params=_params(2),
        name="gqa_attention",
    )(shift, p, *([p] * (2 * N_KV_GQA)))


def _post_kernel(n_streams, *refs):
    oa_ref, ob_ref, mod_ref, wo_ref, nw_ref, wg_ref, wu_ref, wd_ref, out_ref = refs[n_streams:]

    def update(x, rows):
        attn = _dot(oa_ref[0, rows, :], wo_ref[:W_NA, :]) + _dot(ob_ref[0, rows, :], wo_ref[W_NA:, :])
        x = x + mod_ref[0, 2:3, :] * attn
        ms = jnp.mean(x * x, axis=-1, keepdims=True)
        h = x * lax.rsqrt(ms + EPS) * nw_ref[...]
        h = (h * (1.0 + mod_ref[0, 4:5, :]) + mod_ref[0, 3:4, :]).astype(_BF16)
        gate = _dot(h, wg_ref[...])
        up = _dot(h, wu_ref[...])
        act = (gate / (1.0 + jnp.exp(-gate)) * up).astype(_BF16)
        out_ref[0, rows, :] = x + mod_ref[0, 5:6, :] * _dot(act, wd_ref[...])

    _for_stream_rows(refs[:n_streams], update)


def _post(streams, oa, ob, mod_l, wo_bf, norm_w, wg_bf, wu_bf, wd_bf, layer, n_tiles):
    n_rows = n_tiles * TILE
    mod_row = lambda b, t: (jnp.where(t >= N_LAT_STEPS, BATCH, b), 0, 0)
    resident = lambda shape: pl.BlockSpec((None,) + shape, lambda b, t: (layer,) + (0,) * len(shape),
                                          pipeline_mode=pl.Buffered(1))
    rows_of = lambda width: pl.BlockSpec((1, STEP_ROWS, width), lambda b, t: (b, t, 0))
    return pl.pallas_call(
        functools.partial(_post_kernel, len(streams)),
        grid=(BATCH, pl.cdiv(n_rows, STEP_ROWS)),
        in_specs=[
            *_stream_specs(len(streams)),
            rows_of(W_NA),
            rows_of(W_GQA),
            pl.BlockSpec((1, 6, D_MODEL), mod_row),
            resident((W_NA + W_GQA, D_MODEL)),
            resident((1, D_MODEL)),
            resident((D_MODEL, D_FF)),
            resident((D_MODEL, D_FF)),
            resident((D_FF, D_MODEL)),
        ],
        out_specs=rows_of(D_MODEL),
        out_shape=jax.ShapeDtypeStruct((BATCH, n_rows, D_MODEL), _F32),
        compiler_params=_params(2),
        name="post",
    )(*streams, oa, ob, mod_l, wo_bf, norm_w, wg_bf, wu_bf, wd_bf)


def _rope_tables():
    t = np.arange(SEQ)
    half = HEAD_DIM // 2
    inv_freq = np.float32(ROPE_THETA) ** (-np.arange(0, half, 2, dtype=np.float32) / np.float32(half))
    row = (t // GRID_W).astype(np.float32)[:, None] * inv_freq
    col = (t % GRID_W).astype(np.float32)[:, None] * inv_freq
    zeros = np.zeros((SEQ, HEAD_DIM // 4), np.float32)
    cos_head = np.concatenate([np.cos(row), np.cos(row), np.cos(col), np.cos(col)], axis=-1)
    sa_head = np.concatenate([-np.sin(row), zeros, -np.sin(col), zeros], axis=-1)
    sb_head = np.concatenate([zeros, np.sin(row), zeros, np.sin(col)], axis=-1)
    pad = lambda a, fill: jnp.asarray(np.concatenate(
        [np.tile(a, (1, LANES // HEAD_DIM)), np.full((CTX_LEN, LANES), fill, np.float32)],
        axis=0).astype(np.float32))
    return pad(cos_head, 1.0), pad(sa_head, 0.0), pad(sb_head, 0.0)


def _head_gains(qa, ka, qb, kb):
    scale = LOG2_E * HEAD_DIM ** -0.5
    ones = jnp.ones((W_NA,), _F32)
    return jnp.concatenate([
        jnp.tile(qa * scale, N_HEADS_NA), jnp.tile(ka, N_HEADS_NA), ones,
        jnp.tile(qb * scale, N_HEADS_GQA), jnp.tile(kb, N_KV_GQA), jnp.ones((W_KV,), _F32),
    ])[None, :]


def _softmax_shift(q_gain, k_gain, bias_max):
    q_norm_bound = LOG2_E * jnp.max(jnp.abs(q_gain)) * ROUNDING_SLACK
    key_norm_bound = HEAD_DIM ** 0.5 * jnp.max(jnp.abs(k_gain)) * ROUNDING_SLACK
    worst_gap = 2.0 * q_norm_bound * key_norm_bound + bias_max
    return jnp.stack([jnp.where(worst_gap <= SHIFT_RANGE, key_norm_bound, 0.0),
                      jnp.asarray(bias_max, _F32)]).astype(_F32)


def kernel(x, c, ctx, c_ctx, w_ada, b_ada, attn_norm, w_in, q_norm_a, k_norm_a, q_norm_b,
           k_norm_b, rpb, w_out, ffn_norm, w_gate, w_up, w_down):
    assert x.shape == (BATCH, SEQ, D_MODEL) and ctx.shape == (BATCH, CTX_LEN, D_MODEL)
    cond = jnp.concatenate(
        [c, c_ctx[None, :], jnp.zeros((COND_ROWS - BATCH - 1, D_MODEL), _F32)], axis=0)
    mod = _adaln(cond, w_ada, b_ada).reshape(DEPTH, COND_ROWS, 6, D_MODEL)
    bias = _bias_tables(rpb)
    cos_t, sa_t, sb_t = _rope_tables()
    lane = np.arange(LANES)
    ones_bd = jnp.asarray(lane[:, None] // HEAD_DIM == lane[None, :] // HEAD_DIM, _BF16)
    w_out_bf = w_out.astype(_BF16)
    w_gate_bf, w_up_bf, w_down_bf = w_gate.astype(_BF16), w_up.astype(_BF16), w_down.astype(_BF16)

    streams = (x, ctx)
    for l in range(DEPTH):
        last = l == DEPTH - 1
        n_tiles = N_LAT_TILES if last else T_ALL // TILE
        gains = _head_gains(q_norm_a[l], k_norm_a[l], q_norm_b[l], k_norm_b[l])
        p = _inproj(streams, mod[l], attn_norm[l][None, :], w_in, l, gains, ones_bd, cos_t, sa_t, sb_t)
        bias_max = LOG2_E * jnp.maximum(jnp.max(rpb[l]), 0.0)
        oa = _na_attention(_softmax_shift(q_norm_a[l], k_norm_a[l], bias_max), p, bias, l, n_tiles)
        ob = _gqa_attention(_softmax_shift(q_norm_b[l], k_norm_b[l], 0.0), p, n_tiles)
        xc = _post(streams, oa, ob, mod[l], w_out_bf, ffn_norm.reshape(DEPTH, 1, D_MODEL), w_gate_bf,
                   w_up_bf, w_down_bf, l, n_tiles)
        streams = (xc,)
    return xc
```
